```python
import jax, jax.numpy as jnp
from jax import lax
import numpy as np

D_MODEL = 1024
BATCH = 8
SEQ = 8192
DEPTH = 2

CHUNK = 64
MEM_LEN = 256
FOX_HEADS = 8
FOX_HEAD_DIM = 64
FOX_WIDTH = FOX_HEADS * FOX_HEAD_DIM
POOL_WIDTH = D_MODEL - FOX_WIDTH
POOL_WINDOWS = (2, 4, 8, 16)
POOL_GROUPS = len(POOL_WINDOWS)
POOL_GROUP_DIM = POOL_WIDTH // POOL_GROUPS
IN_COLS = 3 * FOX_WIDTH + POOL_WIDTH + FOX_HEADS
MEM_HEADS = 4
MEM_HEAD_DIM = 128
MEM_WIDTH = MEM_HEADS * MEM_HEAD_DIM
D_FF = ((-(-8 * D_MODEL // 3) + 255) // 256) * 256
Q_BLOCK = 128
EPS = 1e-6

kernel_name = "fox_pool_hybrid_encoder"


def rmsnorm(x, g):
    xf = x.astype(jnp.float32)
    y = xf * lax.rsqrt(jnp.mean(xf * xf, axis=-1, keepdims=True) + EPS)
    return (y * g.astype(jnp.float32)).astype(x.dtype)


def forgetting_attention(q, k, v, f_logit):
    c = jnp.cumsum(jax.nn.log_sigmoid(f_logit.astype(jnp.float32)), axis=-1)
    scale = FOX_HEAD_DIM ** -0.5
    S = q.shape[2]
    outs = []
    for start in range(0, S, Q_BLOCK):
        end = start + Q_BLOCK
        qb = q[:, :, start:end]
        kb = k[:, :, :end]
        vb = v[:, :, :end]
        s = jnp.einsum('bhqd,bhkd->bhqk', qb, kb, preferred_element_type=jnp.float32) * scale
        s = s + c[:, :, start:end, None] - c[:, :, None, :end]
        qpos = start + jnp.arange(Q_BLOCK)
        kpos = jnp.arange(end)
        s = jnp.where(kpos[None, :] <= qpos[:, None], s, -jnp.inf)
        p = jax.nn.softmax(s, axis=-1)
        outs.append(jnp.einsum('bhqk,bhkd->bhqd', p.astype(vb.dtype), vb))
    return jnp.concatenate(outs, axis=2)


def multiscale_pool(p, w_pool, pool_scale):
    B, S, _ = p.shape
    pf = p.astype(jnp.float32).reshape(B, S, POOL_GROUPS, POOL_GROUP_DIM)
    cs = jnp.concatenate([jnp.zeros((B, 1, POOL_GROUPS, POOL_GROUP_DIM), jnp.float32),
                          jnp.cumsum(pf, axis=1)], axis=1)
    t = jnp.arange(S)
    outs = []
    for g, w in enumerate(POOL_WINDOWS):
        csg = cs[:, :, g]
        lower = jnp.pad(csg, ((0, 0), (w - 1, 0), (0, 0)))[:, :S]
        cnt = jnp.minimum(t + 1, w).astype(jnp.float32)[None, :, None]
        mixed = (csg[:, 1:] - lower) / cnt - pf[:, :, g]
        outs.append(jnp.einsum('bsc,cd->bsd', mixed, w_pool[g].astype(jnp.float32)))
    y = jnp.concatenate(outs, axis=-1) * pool_scale.astype(jnp.float32)
    return y.astype(p.dtype)


def head_split(t, n_heads, head_dim):
    B, S, _ = t.shape
    return t.reshape(B, S, n_heads, head_dim)


def setup_inputs(seed: int = 0) -> dict:
    key = jax.random.key(seed)
    ks = jax.random.split(key, 24)
    f32 = jnp.float32
    D = D_MODEL

    def nrm(k, shape, fan_in):
        return jax.random.normal(k, shape, f32) * (fan_in ** -0.5)

    def gain(k, shape):
        return 1.0 + 0.05 * jax.random.normal(k, shape, f32)

    return {
        "x": jax.random.normal(ks[0], (BATCH, SEQ, D), f32),
        "mem": jax.random.normal(ks[1], (BATCH, MEM_LEN, D), f32),
        "g_mix": gain(ks[2], (DEPTH, D)),
        "w_in": nrm(ks[3], (DEPTH, D, IN_COLS), D),
        "b_forget": 3.0 + 0.5 * jax.random.normal(ks[4], (DEPTH, FOX_HEADS), f32),
        "g_q_fox": gain(ks[5], (DEPTH, FOX_HEAD_DIM)),
        "g_k_fox": gain(ks[6], (DEPTH, FOX_HEAD_DIM)),
        "w_pool": nrm(ks[7], (DEPTH, POOL_GROUPS, POOL_GROUP_DIM, POOL_GROUP_DIM), POOL_GROUP_DIM),
        "pool_scale": gain(ks[8], (DEPTH, POOL_WIDTH)),
        "w_out": nrm(ks[9], (DEPTH, D, D), D),
        "g_mem_q": gain(ks[10], (DEPTH, D)),
        "g_mem_kv": gain(ks[11], (DEPTH, D)),
        "w_mem_q": nrm(ks[12], (DEPTH, D, MEM_WIDTH), D),
        "w_mem_kv": nrm(ks[13], (DEPTH, D, 2 * MEM_WIDTH), D),
        "g_q_mem": gain(ks[14], (DEPTH, MEM_HEAD_DIM)),
        "g_k_mem": gain(ks[15], (DEPTH, MEM_HEAD_DIM)),
        "w_mem_out": nrm(ks[16], (DEPTH, MEM_WIDTH, D), MEM_WIDTH),
        "g_ffn": gain(ks[17], (DEPTH, D)),
        "w_gate_up": nrm(ks[18], (DEPTH, D, 2 * D_FF), D),
        "w_down": nrm(ks[19], (DEPTH, D_FF, D), D_FF),
    }


def reference(x, mem, g_mix, w_in, b_forget, g_q_fox, g_k_fox, w_pool, pool_scale, w_out,
              g_mem_q, g_mem_kv, w_mem_q, w_mem_kv, g_q_mem, g_k_mem, w_mem_out,
              g_ffn, w_gate_up, w_down):
    B, S, _ = x.shape
    h = x
    for l in range(DEPTH):
        xn = rmsnorm(h, g_mix[l])
        z = jnp.einsum('bsd,dc->bsc', xn, w_in[l])
        q = z[..., :FOX_WIDTH]
        k = z[..., FOX_WIDTH:2 * FOX_WIDTH]
        v = z[..., 2 * FOX_WIDTH:3 * FOX_WIDTH]
        p_in = z[..., 3 * FOX_WIDTH:3 * FOX_WIDTH + POOL_WIDTH]
        f_logit = z[..., 3 * FOX_WIDTH + POOL_WIDTH:] + b_forget[l]
        q = rmsnorm(head_split(q, FOX_HEADS, FOX_HEAD_DIM), g_q_fox[l]).transpose(0, 2, 1, 3)
        k = rmsnorm(head_split(k, FOX_HEADS, FOX_HEAD_DIM), g_k_fox[l]).transpose(0, 2, 1, 3)
        v = head_split(v, FOX_HEADS, FOX_HEAD_DIM).transpose(0, 2, 1, 3)
        fox = forgetting_attention(q, k, v, f_logit.transpose(0, 2, 1))
        fox = fox.transpose(0, 2, 1, 3).reshape(B, S, FOX_WIDTH)
        pool = multiscale_pool(p_in, w_pool[l], pool_scale[l])
        h = h + jnp.einsum('bsc,cd->bsd', jnp.concatenate([fox, pool], axis=-1), w_out[l])

        hn = rmsnorm(h, g_mem_q[l])
        mn = rmsnorm(mem, g_mem_kv[l])
        mq = rmsnorm(head_split(jnp.einsum('bsd,dc->bsc', hn, w_mem_q[l]), MEM_HEADS, MEM_HEAD_DIM), g_q_mem[l])
        mkv = jnp.einsum('bmd,dc->bmc', mn, w_mem_kv[l])
        mk = rmsnorm(head_split(mkv[..., :MEM_WIDTH], MEM_HEADS, MEM_HEAD_DIM), g_k_mem[l])
        mv = head_split(mkv[..., MEM_WIDTH:], MEM_HEADS, MEM_HEAD_DIM)
        sc = jnp.einsum('bshd,bmhd->bhsm', mq, mk, preferred_element_type=jnp.float32) * (MEM_HEAD_DIM ** -0.5)
        pm = jax.nn.softmax(sc, axis=-1).astype(mv.dtype)
        mo = jnp.einsum('bhsm,bmhd->bshd', pm, mv).reshape(B, S, MEM_WIDTH)
        h = h + jnp.einsum('bsc,cd->bsd', mo, w_mem_out[l])

        hn = rmsnorm(h, g_ffn[l])
        gu = jnp.einsum('bsd,df->bsf', hn, w_gate_up[l])
        act = jax.nn.silu(gu[..., :D_FF]) * gu[..., D_FF:]
        h = h + jnp.einsum('bsf,fd->bsd', act, w_down[l])
    return h
```

```python
import functools
import math

import jax
import jax.numpy as jnp
from jax import lax
from jax.experimental import pallas as pl
from jax.experimental.pallas import tpu as pltpu

F32 = jnp.float32
BF16 = jnp.bfloat16

EPS = 1e-6
FOX_HEADS = 8
FOX_HEAD_DIM = 64
FOX_WIDTH = FOX_HEADS * FOX_HEAD_DIM
POOL_WINDOWS = (2, 4, 8, 16)
POOL_GROUP_DIM = 128
POOL_WIDTH = len(POOL_WINDOWS) * POOL_GROUP_DIM
MEM_HEADS = 4
MEM_HEAD_DIM = 128
MEM_WIDTH = MEM_HEADS * MEM_HEAD_DIM

LOG2E = 1.4426950408889634
NEG_BIG = -1e30

QK_AUG = 128
AUG_ROWS = 16
QK_PAD = QK_AUG - FOX_HEAD_DIM - AUG_ROWS
V_ROWS = 80
POOL_HALO = 16

IN_TILE = 512
ATTN_TILE = 256
POST_TILE = 256
FF_CHUNK = 256
VMEM_LIMIT = 52 * 1024 * 1024


def _rms(x, g):
    return x * lax.rsqrt(jnp.mean(x * x, axis=-1, keepdims=True) + EPS) * g


def _split3(x):
    a = x.astype(BF16).astype(F32)
    r = x - a
    b = r.astype(BF16).astype(F32)
    return a, b, r - b


def _const_spec(shape):
    zeros = (0,) * len(shape)
    return pl.BlockSpec(shape, lambda *_: zeros, pipeline_mode=pl.Buffered(1))


def _mem_kv_kernel(mem_ref, g_ref, w_ref, gk_ref, mk_ref, mv_ref):
    xn = _rms(mem_ref[0], g_ref[...]).astype(BF16)
    kv = jnp.dot(xn, w_ref[...], preferred_element_type=F32)
    for h in range(MEM_HEADS):
        lo = h * MEM_HEAD_DIM
        kh = _rms(kv[:, lo:lo + MEM_HEAD_DIM], gk_ref[...])
        mk_ref[0, :, lo:lo + MEM_HEAD_DIM] = kh.astype(BF16)
    mv_ref[0] = kv[:, MEM_WIDTH:].astype(BF16)


def _mem_kv(mem, g, w, gk):
    B, M, D = mem.shape
    return pl.pallas_call(
        _mem_kv_kernel,
        grid=(B,),
        in_specs=[
            pl.BlockSpec((1, M, D), lambda b: (b, 0, 0)),
            _const_spec((1, D)),
            _const_spec((D, 2 * MEM_WIDTH)),
            _const_spec((1, MEM_HEAD_DIM)),
        ],
        out_specs=[
            pl.BlockSpec((1, M, MEM_WIDTH), lambda b: (b, 0, 0)),
            pl.BlockSpec((1, M, MEM_WIDTH), lambda b: (b, 0, 0)),
        ],
        out_shape=[jax.ShapeDtypeStruct((B, M, MEM_WIDTH), BF16)] * 2,
        compiler_params=pltpu.CompilerParams(dimension_semantics=("arbitrary",)),
        name="mem_kv",
    )(mem, g, w, gk)


def _in_proj_kernel(h_ref, g_ref, wt_ref, wp_ref, bf_ref, gq_ref, gk_ref, tri_ref,
                    wpool_ref, pscale_ref,
                    qt_ref, ka_ref, vt_ref, pool_ref,
                    carry_ref, halo_ref, ext_ref):
    tm = h_ref.shape[1]
    si = pl.program_id(1)

    @pl.when(si == 0)
    def _():
        carry_ref[...] = jnp.zeros_like(carry_ref)
        halo_ref[...] = jnp.zeros_like(halo_ref)

    xn = _rms(h_ref[0], g_ref[...]).astype(BF16)
    zt = lax.dot_general(wt_ref[...], xn, (((1,), (1,)), ((), ())),
                         preferred_element_type=F32)

    f = zt[3 * FOX_WIDTH:3 * FOX_WIDTH + FOX_HEADS] + bf_ref[...]
    ls = jnp.minimum(f, 0.0) - jnp.log1p(jnp.exp(-jnp.abs(f)))
    tri = tri_ref[...]
    c = carry_ref[:, 0:1]
    for part in _split3(ls):
        c = c + jnp.dot(part.astype(BF16), tri, preferred_element_type=F32)
    carry_ref[...] = jnp.broadcast_to(c[:, tm - 1:tm], carry_ref.shape)
    c1, c2, c3 = _split3(c * LOG2E)

    row = lax.broadcasted_iota(jnp.int32, (AUG_ROWS, tm), 0)
    ones_row = jnp.where(
        lax.broadcasted_iota(jnp.int32, (V_ROWS - FOX_HEAD_DIM, ATTN_TILE), 0) == 0, 1.0, 0.0
    ).astype(BF16)
    q_scale = FOX_HEAD_DIM ** -0.5 * LOG2E
    for h in range(FOX_HEADS):
        lo = h * FOX_HEAD_DIM
        c1h, c2h, c3h = c1[h:h + 1], c2[h:h + 1], c3[h:h + 1]
        q = zt[lo:lo + FOX_HEAD_DIM]
        q = q * lax.rsqrt(jnp.mean(q * q, axis=0, keepdims=True) + EPS) * (gq_ref[...] * q_scale)
        aug_q = jnp.where(row == 0, c1h, jnp.where(row == 1, c2h, jnp.where(
            row == 2, c3h, jnp.where(row < 6, 1.0, 0.0))))
        qt_ref[0, h, 0:FOX_HEAD_DIM, :] = q.astype(BF16)
        qt_ref[0, h, FOX_HEAD_DIM:FOX_HEAD_DIM + AUG_ROWS, :] = aug_q.astype(BF16)
        qt_ref[0, h, FOX_HEAD_DIM + AUG_ROWS:, :] = jnp.zeros((QK_PAD, tm), BF16)
        k = zt[FOX_WIDTH + lo:FOX_WIDTH + lo + FOX_HEAD_DIM]
        k = k * lax.rsqrt(jnp.mean(k * k, axis=0, keepdims=True) + EPS) * gk_ref[...]
        aug_k = jnp.where(row < 3, 1.0, jnp.where(row == 3, -c1h, jnp.where(
            row == 4, -c2h, jnp.where(row == 5, -c3h, 0.0))))
        k_full = jnp.concatenate(
            [k, aug_k, jnp.zeros((QK_PAD, tm), F32)], axis=0)
        ka_ref[0, h] = k_full.T.astype(BF16)
        v = zt[2 * FOX_WIDTH + lo:2 * FOX_WIDTH + lo + FOX_HEAD_DIM]
        for jj in range(tm // ATTN_TILE):
            sl = slice(jj * ATTN_TILE, (jj + 1) * ATTN_TILE)
            vt_ref[0, h, jj, 0:FOX_HEAD_DIM, :] = v[:, sl].astype(BF16)
            vt_ref[0, h, jj, FOX_HEAD_DIM:, :] = ones_row

    p = jnp.dot(xn, wp_ref[...], preferred_element_type=F32)
    ext_ref[0:POOL_HALO, :] = halo_ref[...]
    ext_ref[POOL_HALO:POOL_HALO + tm, :] = p
    halo_ref[...] = p[tm - POOL_HALO:tm, :]
    t_pos = si * tm + lax.broadcasted_iota(jnp.int32, (tm, POOL_GROUP_DIM), 0)
    for g, w in enumerate(POOL_WINDOWS):
        lo = g * POOL_GROUP_DIM
        acc = p[:, lo:lo + POOL_GROUP_DIM]
        for j in range(1, w):
            acc = acc + ext_ref[POOL_HALO - j:POOL_HALO - j + tm, lo:lo + POOL_GROUP_DIM]
        cnt = jnp.minimum(t_pos + 1, w).astype(F32)
        mixed = acc / cnt - p[:, lo:lo + POOL_GROUP_DIM]
        y = jnp.dot(mixed.astype(BF16), wpool_ref[g], preferred_element_type=F32)
        pool_ref[0, :, lo:lo + POOL_GROUP_DIM] = (
            y * pscale_ref[:, lo:lo + POOL_GROUP_DIM]).astype(BF16)


def _in_proj(h, g, wt, wp, bf, gq, gk, tri, wpool, pscale):
    B, S, D = h.shape
    tm = IN_TILE
    n_sub = tm // ATTN_TILE
    out_shape = [
        jax.ShapeDtypeStruct((B, FOX_HEADS, QK_AUG, S), BF16),
        jax.ShapeDtypeStruct((B, FOX_HEADS, S, QK_AUG), BF16),
        jax.ShapeDtypeStruct((B, FOX_HEADS, S // ATTN_TILE, V_ROWS, ATTN_TILE), BF16),
        jax.ShapeDtypeStruct((B, S, POOL_WIDTH), BF16),
    ]
    return pl.pallas_call(
        _in_proj_kernel,
        grid=(B, S // tm),
        in_specs=[
            pl.BlockSpec((1, tm, D), lambda b, s: (b, s, 0)),
            _const_spec((1, D)),
            _const_spec(wt.shape),
            _const_spec(wp.shape),
            _const_spec(bf.shape),
            _const_spec(gq.shape),
            _const_spec(gk.shape),
            _const_spec(tri.shape),
            _const_spec(wpool.shape),
            _const_spec(pscale.shape),
        ],
        out_specs=[
            pl.BlockSpec((1, FOX_HEADS, QK_AUG, tm), lambda b, s: (b, 0, 0, s)),
            pl.BlockSpec((1, FOX_HEADS, tm, QK_AUG), lambda b, s: (b, 0, s, 0)),
            pl.BlockSpec((1, FOX_HEADS, n_sub, V_ROWS, ATTN_TILE), lambda b, s: (b, 0, s, 0, 0)),
            pl.BlockSpec((1, tm, POOL_WIDTH), lambda b, s: (b, s, 0)),
        ],
        out_shape=out_shape,
        scratch_shapes=[
            pltpu.VMEM((FOX_HEADS, 128), F32),
            pltpu.VMEM((POOL_HALO, POOL_WIDTH), F32),
            pltpu.VMEM((POOL_HALO + tm, POOL_WIDTH), F32),
        ],
        compiler_params=pltpu.CompilerParams(
            dimension_semantics=("arbitrary", "arbitrary"), vmem_limit_bytes=VMEM_LIMIT),
        name="in_proj",
    )(h, g, wt, wp, bf, gq, gk, tri, wpool, pscale)


def _fox_kernel(qt_ref, ka_ref, vt_ref, o_ref):
    t = qt_ref.shape[3]
    i = pl.program_id(2)
    q = qt_ref[0, 0]

    def step(j, carry, masked):
        m, acc = carry
        k = ka_ref[0, 0, pl.ds(pl.multiple_of(j * t, t), t), :]
        s = jnp.dot(k, q, preferred_element_type=F32)
        if masked:
            key = lax.broadcasted_iota(jnp.int32, (t, t), 0)
            qry = lax.broadcasted_iota(jnp.int32, (t, t), 1)
            s = jnp.where(key <= qry, s, NEG_BIG)
        m_new = jnp.maximum(m, jnp.max(s, axis=0, keepdims=True))
        alpha = jnp.exp2(m - m_new)
        p = jnp.exp2(s - m_new).astype(BF16)
        acc = acc * alpha + jnp.dot(vt_ref[0, 0, j], p, preferred_element_type=F32)
        return m_new, acc

    m0 = jnp.full((1, t), NEG_BIG, F32)
    acc0 = jnp.zeros((V_ROWS, t), F32)
    m, acc = lax.fori_loop(0, i, functools.partial(step, masked=False), (m0, acc0))
    m, acc = step(i, (m, acc), masked=True)
    o_ref[0, 0] = (acc[0:FOX_HEAD_DIM] / acc[FOX_HEAD_DIM:FOX_HEAD_DIM + 1]).astype(BF16)


def _fox_attn(qt, ka, vt):
    B, H, _, S = qt.shape
    t = ATTN_TILE
    return pl.pallas_call(
        _fox_kernel,
        grid=(B, H, S // t),
        in_specs=[
            pl.BlockSpec((1, 1, QK_AUG, t), lambda b, h, i: (b, h, 0, i)),
            pl.BlockSpec((1, 1, S, QK_AUG), lambda b, h, i: (b, h, 0, 0)),
            pl.BlockSpec((1, 1, S // t, V_ROWS, t), lambda b, h, i: (b, h, 0, 0, 0)),
        ],
        out_specs=pl.BlockSpec((1, 1, FOX_HEAD_DIM, t), lambda b, h, i: (b, h, 0, i)),
        out_shape=jax.ShapeDtypeStruct((B, H, FOX_HEAD_DIM, S), BF16),
        compiler_params=pltpu.CompilerParams(
            dimension_semantics=("arbitrary", "arbitrary", "arbitrary"),
            vmem_limit_bytes=VMEM_LIMIT),
        name="fox_attn",
    )(qt, ka, vt)


def _post_kernel(h_ref, fox_ref, pool_ref, wout_ref, gmq_ref, wmq_ref, gqm_ref, mk_ref, mv_ref,
                 wmo_ref, gffn_ref, wgu_ref, wd_ref, o_ref, act_ref):
    d_ff = wd_ref.shape[0]
    mix = jnp.concatenate([fox_ref[0], pool_ref[0]], axis=-1)
    h = h_ref[0] + jnp.dot(mix, wout_ref[...], preferred_element_type=F32)

    hn = _rms(h, gmq_ref[...]).astype(BF16)
    mq = jnp.dot(hn, wmq_ref[...], preferred_element_type=F32)
    heads = []
    for hd in range(MEM_HEADS):
        lo = hd * MEM_HEAD_DIM
        qh = _rms(mq[:, lo:lo + MEM_HEAD_DIM], gqm_ref[...] * (MEM_HEAD_DIM ** -0.5 * LOG2E))
        sc = lax.dot_general(qh.astype(BF16), mk_ref[0, :, lo:lo + MEM_HEAD_DIM],
                             (((1,), (1,)), ((), ())), preferred_element_type=F32)
        e = jnp.exp2(sc - jnp.max(sc, axis=-1, keepdims=True))
        pm = (e / jnp.sum(e, axis=-1, keepdims=True)).astype(BF16)
        heads.append(jnp.dot(pm, mv_ref[0, :, lo:lo + MEM_HEAD_DIM],
                             preferred_element_type=F32).astype(BF16))
    mo = jnp.concatenate(heads, axis=-1)
    h = h + jnp.dot(mo, wmo_ref[...], preferred_element_type=F32)

    hn = _rms(h, gffn_ref[...]).astype(BF16)
    for c in range(d_ff // FF_CHUNK):
        lo = c * FF_CHUNK
        gate = jnp.dot(hn, wgu_ref[:, lo:lo + FF_CHUNK], preferred_element_type=F32)
        up = jnp.dot(hn, wgu_ref[:, d_ff + lo:d_ff + lo + FF_CHUNK], preferred_element_type=F32)
        act_ref[:, lo:lo + FF_CHUNK] = (gate * jax.nn.sigmoid(gate) * up).astype(BF16)
    o_ref[0] = h + jnp.dot(act_ref[...], wd_ref[...], preferred_element_type=F32)


def _post(h, fox, pool, wout, gmq, wmq, gqm, mk, mv, wmo, gffn, wgu, wd):
    B, S, D = h.shape
    tm = POST_TILE
    M = mk.shape[1]
    d_ff = wd.shape[0]
    row = lambda b, s: (b, s, 0)
    per_b = lambda b, s: (b, 0, 0)
    return pl.pallas_call(
        _post_kernel,
        grid=(B, S // tm),
        in_specs=[
            pl.BlockSpec((1, tm, D), row),
            pl.BlockSpec((1, tm, FOX_WIDTH), row),
            pl.BlockSpec((1, tm, POOL_WIDTH), row),
            _const_spec(wout.shape),
            _const_spec(gmq.shape),
            _const_spec(wmq.shape),
            _const_spec(gqm.shape),
            pl.BlockSpec((1, M, MEM_WIDTH), per_b),
            pl.BlockSpec((1, M, MEM_WIDTH), per_b),
            _const_spec(wmo.shape),
            _const_spec(gffn.shape),
            _const_spec(wgu.shape),
            _const_spec(wd.shape),
        ],
        out_specs=pl.BlockSpec((1, tm, D), row),
        out_shape=jax.ShapeDtypeStruct((B, S, D), F32),
        scratch_shapes=[pltpu.VMEM((tm, d_ff), BF16)],
        compiler_params=pltpu.CompilerParams(
            dimension_semantics=("arbitrary", "arbitrary"), vmem_limit_bytes=VMEM_LIMIT),
        name="post",
    )(h, fox, pool, wout, gmq, wmq, gqm, mk, mv, wmo, gffn, wgu, wd)


def kernel(x, mem, g_mix, w_in, b_forget, g_q_fox, g_k_fox, w_pool, pool_scale, w_out,
           g_mem_q, g_mem_kv, w_mem_q, w_mem_kv, g_q_mem, g_k_mem, w_mem_out,
           g_ffn, w_gate_up, w_down):
    B, S, D = x.shape
    depth = g_mix.shape[0]
    assert S % IN_TILE == 0 and S % POST_TILE == 0 and IN_TILE % ATTN_TILE == 0
    assert w_down.shape[1] % FF_CHUNK == 0

    idx = jnp.arange(IN_TILE)
    tri = (idx[:, None] <= idx[None, :]).astype(BF16)

    h = x
    for l in range(depth):
        qkv_cols = 3 * FOX_WIDTH
        w_p = w_in[l][:, qkv_cols:qkv_cols + POOL_WIDTH].astype(BF16)
        w_t = jnp.concatenate(
            [w_in[l][:, :qkv_cols], w_in[l][:, qkv_cols + POOL_WIDTH:]], axis=1).T.astype(BF16)

        qt, ka, vt, pool = _in_proj(
            h, g_mix[l][None, :], w_t, w_p, b_forget[l][:, None],
            g_q_fox[l][:, None], g_k_fox[l][:, None], tri,
            w_pool[l].astype(BF16), pool_scale[l][None, :])
        fox_t = _fox_attn(qt, ka, vt)
        fox = fox_t.transpose(0, 3, 1, 2).reshape(B, S, FOX_WIDTH)
        mk, mv = _mem_kv(mem, g_mem_kv[l][None, :], w_mem_kv[l].astype(BF16), g_k_mem[l][None, :])
        h = _post(h, fox, pool, w_out[l].astype(BF16), g_mem_q[l][None, :],
                  w_mem_q[l].astype(BF16), g_q_mem[l][None, :], mk, mv,
                  w_mem_out[l].astype(BF16), g_ffn[l][None, :],
                  w_gate_up[l].astype(BF16), w_down[l].astype(BF16))
    return h
```

```python
import functools
import math

import jax
import jax.numpy as jnp
from jax import lax
from jax.experimental import pallas as pl
from jax.experimental.pallas import tpu as pltpu

F32 = jnp.float32
BF16 = jnp.bfloat16

EPS = 1e-6
FOX_HEADS = 8
FOX_HEAD_DIM = 64
FOX_WIDTH = FOX_HEADS * FOX_HEAD_DIM
POOL_WINDOWS = (2, 4, 8, 16)
POOL_GROUP_DIM = 128
POOL_WIDTH = len(POOL_WINDOWS) * POOL_GROUP_DIM
MEM_HEADS = 4
MEM_HEAD_DIM = 128
MEM_WIDTH = MEM_HEADS * MEM_HEAD_DIM

LOG2E = 1.4426950408889634
NEG_BIG = -1e30

QK_AUG = 128
AUG_ROWS = 16
QK_PAD = QK_AUG - FOX_HEAD_DIM - AUG_ROWS
V_ROWS = 80
POOL_HALO = 16

IN_TILE = 512
ATTN_TILE = 256
POST_TILE = 256
FF_CHUNK = 256
VMEM_LIMIT = 52 * 1024 * 1024


def _rms(x, g):
    return x * lax.rsqrt(jnp.mean(x * x, axis=-1, keepdims=True) + EPS) * g


def _split3(x):
    a = x.astype(BF16).astype(F32)
    r = x - a
    b = r.astype(BF16).astype(F32)
    return a, b, r - b


def _const_spec(shape):
    zeros = (0,) * len(shape)
    return pl.BlockSpec(shape, lambda *_: zeros, pipeline_mode=pl.Buffered(1))


def _mem_kv_kernel(mem_ref, g_ref, w_ref, gk_ref, mk_ref, mv_ref):
    xn = _rms(mem_ref[0], g_ref[...]).astype(BF16)
    kv = jnp.dot(xn, w_ref[...], preferred_element_type=F32)
    for h in range(MEM_HEADS):
        lo = h * MEM_HEAD_DIM
        kh = _rms(kv[:, lo:lo + MEM_HEAD_DIM], gk_ref[...])
        mk_ref[0, :, lo:lo + MEM_HEAD_DIM] = kh.astype(BF16)
    mv_ref[0] = kv[:, MEM_WIDTH:].astype(BF16)


def _mem_kv(mem, g, w, gk):
    B, M, D = mem.shape
    return pl.pallas_call(
        _mem_kv_kernel,
        grid=(B,),
        in_specs=[
            pl.BlockSpec((1, M, D), lambda b: (b, 0, 0)),
            _const_spec((1, D)),
            _const_spec((D, 2 * MEM_WIDTH)),
            _const_spec((1, MEM_HEAD_DIM)),
        ],
        out_specs=[
            pl.BlockSpec((1, M, MEM_WIDTH), lambda b: (b, 0, 0)),
            pl.BlockSpec((1, M, MEM_WIDTH), lambda b: (b, 0, 0)),
        ],
        out_shape=[jax.ShapeDtypeStruct((B, M, MEM_WIDTH), BF16)] * 2,
        compiler_params=pltpu.CompilerParams(dimension_semantics=("arbitrary",)),
        name="mem_kv",
    )(mem, g, w, gk)


def _in_proj_kernel(h_ref, g_ref, wt_ref, wp_ref, bf_ref, gq_ref, gk_ref, tri_ref,
                    wpool_ref, pscale_ref,
                    qt_ref, ka_ref, vt_ref, pool_ref,
                    carry_ref, halo_ref, ext_ref):
    tm = h_ref.shape[1]
    si = pl.program_id(1)

    @pl.when(si == 0)
    def _():
        carry_ref[...] = jnp.zeros_like(carry_ref)
        halo_ref[...] = jnp.zeros_like(halo_ref)

    xn = _rms(h_ref[0], g_ref[...]).astype(BF16)
    zt = lax.dot_general(wt_ref[...], xn, (((1,), (1,)), ((), ())),
                         preferred_element_type=F32)

    f = zt[3 * FOX_WIDTH:3 * FOX_WIDTH + FOX_HEADS] + bf_ref[...]
    ls = jnp.minimum(f, 0.0) - jnp.log1p(jnp.exp(-jnp.abs(f)))
    tri = tri_ref[...]
    c = carry_ref[:, 0:1]
    for part in _split3(ls):
        c = c + jnp.dot(part.astype(BF16), tri, preferred_element_type=F32)
    carry_ref[...] = jnp.broadcast_to(c[:, tm - 1:tm], carry_ref.shape)
    c1, c2, c3 = _split3(c * LOG2E)

    row = lax.broadcasted_iota(jnp.int32, (AUG_ROWS, tm), 0)
    ones_row = jnp.where(
        lax.broadcasted_iota(jnp.int32, (V_ROWS - FOX_HEAD_DIM, ATTN_TILE), 0) == 0, 1.0, 0.0
    ).astype(BF16)
    q_scale = FOX_HEAD_DIM ** -0.5 * LOG2E
    for h in range(FOX_HEADS):
        lo = h * FOX_HEAD_DIM
        c1h, c2h, c3h = c1[h:h + 1], c2[h:h + 1], c3[h:h + 1]
        q = zt[lo:lo + FOX_HEAD_DIM]
        q = q * lax.rsqrt(jnp.mean(q * q, axis=0, keepdims=True) + EPS) * (gq_ref[...] * q_scale)
        aug_q = jnp.where(row == 0, c1h, jnp.where(row == 1, c2h, jnp.where(
            row == 2, c3h, jnp.where(row < 6, 1.0, 0.0))))
        qt_ref[0, h, 0:FOX_HEAD_DIM, :] = q.astype(BF16)
        qt_ref[0, h, FOX_HEAD_DIM:FOX_HEAD_DIM + AUG_ROWS, :] = aug_q.astype(BF16)
        qt_ref[0, h, FOX_HEAD_DIM + AUG_ROWS:, :] = jnp.zeros((QK_PAD, tm), BF16)
        k = zt[FOX_WIDTH + lo:FOX_WIDTH + lo + FOX_HEAD_DIM]
        k = k * lax.rsqrt(jnp.mean(k * k, axis=0, keepdims=True) + EPS) * gk_ref[...]
        aug_k = jnp.where(row < 3, 1.0, jnp.where(row == 3, -c1h, jnp.where(
            row == 4, -c2h, jnp.where(row == 5, -c3h, 0.0))))
        k_full = jnp.concatenate(
            [k, aug_k, jnp.zeros((QK_PAD, tm), F32)], axis=0)
        ka_ref[0, h] = k_full.T.astype(BF16)
        v = zt[2 * FOX_WIDTH + lo:2 * FOX_WIDTH + lo + FOX_HEAD_DIM]
        for jj in range(tm // ATTN_TILE):
            sl = slice(jj * ATTN_TILE, (jj + 1) * ATTN_TILE)
            vt_ref[0, h, jj, 0:FOX_HEAD_DIM, :] = v[:, sl].astype(BF16)
            vt_ref[0, h, jj, FOX_HEAD_DIM:, :] = ones_row

    p = jnp.dot(xn, wp_ref[...], preferred_element_type=F32)
    ext_ref[0:POOL_HALO, :] = halo_ref[...]
    ext_ref[POOL_HALO:POOL_HALO + tm, :] = p
    halo_ref[...] = p[tm - POOL_HALO:tm, :]
    t_pos = si * tm + lax.broadcasted_iota(jnp.int32, (tm, POOL_GROUP_DIM), 0)
    for g, w in enumerate(POOL_WINDOWS):
        lo = g * POOL_GROUP_DIM
        acc = p[:, lo:lo + POOL_GROUP_DIM]
        for j in range(1, w):
            acc = acc + ext_ref[POOL_HALO - j:POOL_HALO - j + tm, lo:lo + POOL_GROUP_DIM]
        cnt = jnp.minimum(t_pos + 1, w).astype(F32)
        mixed = acc / cnt - p[:, lo:lo + POOL_GROUP_DIM]
        y = jnp.dot(mixed.astype(BF16), wpool_ref[g], preferred_element_type=F32)
        pool_ref[0, :, lo:lo + POOL_GROUP_DIM] = (
            y * pscale_ref[:, lo:lo + POOL_GROUP_DIM]).astype(BF16)


def _in_proj(h, g, wt, wp, bf, gq, gk, tri, wpool, pscale):
    B, S, D = h.shape
    tm = IN_TILE
    n_sub = tm // ATTN_TILE
    out_shape = [
        jax.ShapeDtypeStruct((B, FOX_HEADS, QK_AUG, S), BF16),
        jax.ShapeDtypeStruct((B, FOX_HEADS, S, QK_AUG), BF16),
        jax.ShapeDtypeStruct((B, FOX_HEADS, S // ATTN_TILE, V_ROWS, ATTN_TILE), BF16),
        jax.ShapeDtypeStruct((B, S, POOL_WIDTH), BF16),
    ]
    return pl.pallas_call(
        _in_proj_kernel,
        grid=(B, S // tm),
        in_specs=[
            pl.BlockSpec((1, tm, D), lambda b, s: (b, s, 0)),
            _const_spec((1, D)),
            _const_spec(wt.shape),
            _const_spec(wp.shape),
            _const_spec(bf.shape),
            _const_spec(gq.shape),
            _const_spec(gk.shape),
            _const_spec(tri.shape),
            _const_spec(wpool.shape),
            _const_spec(pscale.shape),
        ],
        out_specs=[
            pl.BlockSpec((1, FOX_HEADS, QK_AUG, tm), lambda b, s: (b, 0, 0, s)),
            pl.BlockSpec((1, FOX_HEADS, tm, QK_AUG), lambda b, s: (b, 0, s, 0)),
            pl.BlockSpec((1, FOX_HEADS, n_sub, V_ROWS, ATTN_TILE), lambda b, s: (b, 0, s, 0, 0)),
            pl.BlockSpec((1, tm, POOL_WIDTH), lambda b, s: (b, s, 0)),
        ],
        out_shape=out_shape,
        scratch_shapes=[
            pltpu.VMEM((FOX_HEADS, 128), F32),
            pltpu.VMEM((POOL_HALO, POOL_WIDTH), F32),
            pltpu.VMEM((POOL_HALO + tm, POOL_WIDTH), F32),
        ],
        compiler_params=pltpu.CompilerParams(
            dimension_semantics=("arbitrary", "arbitrary"), vmem_limit_bytes=VMEM_LIMIT),
        name="in_proj",
    )(h, g, wt, wp, bf, gq, gk, tri, wpool, pscale)


def _fox_kernel(qt_ref, ka_ref, vt_ref, o_ref, m_ref, alpha_ref, acc_ref, s_ref):
    n_heads = qt_ref.shape[1]
    t = qt_ref.shape[3]
    i = pl.program_id(1)

    m_ref[...] = jnp.full(m_ref.shape, NEG_BIG, F32)
    acc_ref[...] = jnp.zeros(acc_ref.shape, F32)

    def step(j, masked):
        for h in range(n_heads):
            k = ka_ref[0, h, pl.ds(pl.multiple_of(j * t, t), t), :]
            s = jnp.dot(k, qt_ref[0, h], preferred_element_type=F32)
            if masked:
                key = lax.broadcasted_iota(jnp.int32, (t, t), 0)
                qry = lax.broadcasted_iota(jnp.int32, (t, t), 1)
                s = jnp.where(key <= qry, s, NEG_BIG)
            s_ref[h] = s
            m_old = m_ref[h]
            m_new = jnp.maximum(m_old, jnp.max(s, axis=0, keepdims=True))
            alpha_ref[h] = jnp.exp2(m_old - m_new)
            m_ref[h] = m_new
        for h in range(n_heads):
            p = jnp.exp2(s_ref[h] - m_ref[h]).astype(BF16)
            acc_ref[h] = acc_ref[h] * alpha_ref[h] + jnp.dot(
                vt_ref[0, h, j], p, preferred_element_type=F32)

    def body(j, carry):
        step(j, masked=False)
        return carry

    lax.fori_loop(0, i, body, 0)
    step(i, masked=True)

    for hp in range(n_heads // 2):
        pair = []
        for h in (2 * hp, 2 * hp + 1):
            acc = acc_ref[h]
            pair.append(acc[0:FOX_HEAD_DIM] / acc[FOX_HEAD_DIM:FOX_HEAD_DIM + 1])
        lo = 2 * hp * FOX_HEAD_DIM
        o_ref[0, :, lo:lo + 2 * FOX_HEAD_DIM] = jnp.concatenate(pair, axis=0).T.astype(BF16)


def _fox_attn(qt, ka, vt):
    B, H, _, S = qt.shape
    t = ATTN_TILE
    return pl.pallas_call(
        _fox_kernel,
        grid=(B, S // t),
        in_specs=[
            pl.BlockSpec((1, H, QK_AUG, t), lambda b, i: (b, 0, 0, i)),
            pl.BlockSpec((1, H, S, QK_AUG), lambda b, i: (b, 0, 0, 0), pipeline_mode=pl.Buffered(1)),
            pl.BlockSpec((1, H, S // t, V_ROWS, t), lambda b, i: (b, 0, 0, 0, 0),
                         pipeline_mode=pl.Buffered(1)),
        ],
        out_specs=pl.BlockSpec((1, t, H * FOX_HEAD_DIM), lambda b, i: (b, i, 0)),
        out_shape=jax.ShapeDtypeStruct((B, S, H * FOX_HEAD_DIM), BF16),
        scratch_shapes=[
            pltpu.VMEM((H, 1, t), F32),
            pltpu.VMEM((H, 1, t), F32),
            pltpu.VMEM((H, V_ROWS, t), F32),
            pltpu.VMEM((H, t, t), F32),
        ],
        compiler_params=pltpu.CompilerParams(
            dimension_semantics=("arbitrary", "arbitrary"), vmem_limit_bytes=VMEM_LIMIT),
        name="fox_attn",
    )(qt, ka, vt)


def _post_kernel(h_ref, fox_ref, pool_ref, wout_ref, gmq_ref, wmq_ref, gqm_ref, mk_ref, mv_ref,
                 wmo_ref, gffn_ref, wgu_ref, wd_ref, o_ref, act_ref):
    d_ff = wd_ref.shape[0]
    mix = jnp.concatenate([fox_ref[0], pool_ref[0]], axis=-1)
    h = h_ref[0] + jnp.dot(mix, wout_ref[...], preferred_element_type=F32)

    hn = _rms(h, gmq_ref[...]).astype(BF16)
    mq = jnp.dot(hn, wmq_ref[...], preferred_element_type=F32)
    heads = []
    for hd in range(MEM_HEADS):
        lo = hd * MEM_HEAD_DIM
        qh = _rms(mq[:, lo:lo + MEM_HEAD_DIM], gqm_ref[...] * (MEM_HEAD_DIM ** -0.5 * LOG2E))
        sc = lax.dot_general(qh.astype(BF16), mk_ref[0, :, lo:lo + MEM_HEAD_DIM],
                             (((1,), (1,)), ((), ())), preferred_element_type=F32)
        e = jnp.exp2(sc - jnp.max(sc, axis=-1, keepdims=True))
        pm = (e / jnp.sum(e, axis=-1, keepdims=True)).astype(BF16)
        heads.append(jnp.dot(pm, mv_ref[0, :, lo:lo + MEM_HEAD_DIM],
                             preferred_element_type=F32).astype(BF16))
    mo = jnp.concatenate(heads, axis=-1)
    h = h + jnp.dot(mo, wmo_ref[...], preferred_element_type=F32)

    hn = _rms(h, gffn_ref[...]).astype(BF16)
    for c in range(d_ff // FF_CHUNK):
        lo = c * FF_CHUNK
        gate = jnp.dot(hn, wgu_ref[:, lo:lo + FF_CHUNK], preferred_element_type=F32)
        up = jnp.dot(hn, wgu_ref[:, d_ff + lo:d_ff + lo + FF_CHUNK], preferred_element_type=F32)
        act_ref[:, lo:lo + FF_CHUNK] = (gate * jax.nn.sigmoid(gate) * up).astype(BF16)
    o_ref[0] = h + jnp.dot(act_ref[...], wd_ref[...], preferred_element_type=F32)


def _post(h, fox, pool, wout, gmq, wmq, gqm, mk, mv, wmo, gffn, wgu, wd):
    B, S, D = h.shape
    tm = POST_TILE
    M = mk.shape[1]
    d_ff = wd.shape[0]
    row = lambda b, s: (b, s, 0)
    per_b = lambda b, s: (b, 0, 0)
    return pl.pallas_call(
        _post_kernel,
        grid=(B, S // tm),
        in_specs=[
            pl.BlockSpec((1, tm, D), row),
            pl.BlockSpec((1, tm, FOX_WIDTH), row),
            pl.BlockSpec((1, tm, POOL_WIDTH), row),
            _const_spec(wout.shape),
            _const_spec(gmq.shape),
            _const_spec(wmq.shape),
            _const_spec(gqm.shape),
            pl.BlockSpec((1, M, MEM_WIDTH), per_b),
            pl.BlockSpec((1, M, MEM_WIDTH), per_b),
            _const_spec(wmo.shape),
            _const_spec(gffn.shape),
            _const_spec(wgu.shape),
            _const_spec(wd.shape),
        ],
        out_specs=pl.BlockSpec((1, tm, D), row),
        out_shape=jax.ShapeDtypeStruct((B, S, D), F32),
        scratch_shapes=[pltpu.VMEM((tm, d_ff), BF16)],
        compiler_params=pltpu.CompilerParams(
            dimension_semantics=("arbitrary", "arbitrary"), vmem_limit_bytes=VMEM_LIMIT),
        name="post",
    )(h, fox, pool, wout, gmq, wmq, gqm, mk, mv, wmo, gffn, wgu, wd)


def kernel(x, mem, g_mix, w_in, b_forget, g_q_fox, g_k_fox, w_pool, pool_scale, w_out,
           g_mem_q, g_mem_kv, w_mem_q, w_mem_kv, g_q_mem, g_k_mem, w_mem_out,
           g_ffn, w_gate_up, w_down):
    B, S, D = x.shape
    depth = g_mix.shape[0]
    assert S % IN_TILE == 0 and S % POST_TILE == 0 and IN_TILE % ATTN_TILE == 0
    assert w_down.shape[1] % FF_CHUNK == 0

    idx = jnp.arange(IN_TILE)
    tri = (idx[:, None] <= idx[None, :]).astype(BF16)

    h = x
    for l in range(depth):
        qkv_cols = 3 * FOX_WIDTH
        w_p = w_in[l][:, qkv_cols:qkv_cols + POOL_WIDTH].astype(BF16)
        w_t = jnp.concatenate(
            [w_in[l][:, :qkv_cols], w_in[l][:, qkv_cols + POOL_WIDTH:]], axis=1).T.astype(BF16)

        qt, ka, vt, pool = _in_proj(
            h, g_mix[l][None, :], w_t, w_p, b_forget[l][:, None],
            g_q_fox[l][:, None], g_k_fox[l][:, None], tri,
            w_pool[l].astype(BF16), pool_scale[l][None, :])
        fox = _fox_attn(qt, ka, vt)
        mk, mv = _mem_kv(mem, g_mem_kv[l][None, :], w_mem_kv[l].astype(BF16), g_k_mem[l][None, :])
        h = _post(h, fox, pool, w_out[l].astype(BF16), g_mem_q[l][None, :],
                  w_mem_q[l].astype(BF16), g_q_mem[l][None, :], mk, mv,
                  w_mem_out[l].astype(BF16), g_ffn[l][None, :],
                  w_gate_up[l].astype(BF16), w_down[l].astype(BF16))
    return h
```

```python
import functools
import math

import jax
import jax.numpy as jnp
from jax import lax
from jax.experimental import pallas as pl
from jax.experimental.pallas import tpu as pltpu

F32 = jnp.float32
BF16 = jnp.bfloat16

EPS = 1e-6
FOX_HEADS = 8
FOX_HEAD_DIM = 64
FOX_WIDTH = FOX_HEADS * FOX_HEAD_DIM
POOL_WINDOWS = (2, 4, 8, 16)
POOL_GROUP_DIM = 128
POOL_WIDTH = len(POOL_WINDOWS) * POOL_GROUP_DIM
MEM_HEADS = 4
MEM_HEAD_DIM = 128
MEM_WIDTH = MEM_HEADS * MEM_HEAD_DIM

LOG2E = 1.4426950408889634
NEG_BIG = -1e30

QK_AUG = 128
AUG_ROWS = 16
QK_PAD = QK_AUG - FOX_HEAD_DIM - AUG_ROWS
V_ROWS = 80
POOL_HALO = 16

IN_TILE = 512
ATTN_TILE = 256
POST_TILE = 256
FF_CHUNK = 256
VMEM_LIMIT = 52 * 1024 * 1024


def _rms(x, g):
    return x * lax.rsqrt(jnp.mean(x * x, axis=-1, keepdims=True) + EPS) * g


def _split3(x):
    a = x.astype(BF16).astype(F32)
    r = x - a
    b = r.astype(BF16).astype(F32)
    return a, b, r - b


def _const_spec(shape):
    zeros = (0,) * len(shape)
    return pl.BlockSpec(shape, lambda *_: zeros, pipeline_mode=pl.Buffered(1))


def _mem_kv_kernel(mem_ref, g_ref, w_ref, gk_ref, mk_ref, mv_ref):
    xn = _rms(mem_ref[0], g_ref[...]).astype(BF16)
    kv = jnp.dot(xn, w_ref[...], preferred_element_type=F32)
    for h in range(MEM_HEADS):
        lo = h * MEM_HEAD_DIM
        kh = _rms(kv[:, lo:lo + MEM_HEAD_DIM], gk_ref[...])
        mk_ref[0, :, lo:lo + MEM_HEAD_DIM] = kh.astype(BF16)
    mv_ref[0] = kv[:, MEM_WIDTH:].astype(BF16)


def _mem_kv(mem, g, w, gk):
    B, M, D = mem.shape
    return pl.pallas_call(
        _mem_kv_kernel,
        grid=(B,),
        in_specs=[
            pl.BlockSpec((1, M, D), lambda b: (b, 0, 0)),
            _const_spec((1, D)),
            _const_spec((D, 2 * MEM_WIDTH)),
            _const_spec((1, MEM_HEAD_DIM)),
        ],
        out_specs=[
            pl.BlockSpec((1, M, MEM_WIDTH), lambda b: (b, 0, 0)),
            pl.BlockSpec((1, M, MEM_WIDTH), lambda b: (b, 0, 0)),
        ],
        out_shape=[jax.ShapeDtypeStruct((B, M, MEM_WIDTH), BF16)] * 2,
        compiler_params=pltpu.CompilerParams(dimension_semantics=("arbitrary",)),
        name="mem_kv",
    )(mem, g, w, gk)


def _in_proj_kernel(h_ref, g_ref, wt_ref, wp_ref, bf_ref, gq_ref, gk_ref, tri_ref,
                    wpool_ref, pscale_ref,
                    qt_ref, ka_ref, vt_ref, pool_ref,
                    carry_ref, halo_ref, ext_ref):
    tm = h_ref.shape[1]
    si = pl.program_id(1)

    @pl.when(si == 0)
    def _():
        carry_ref[...] = jnp.zeros_like(carry_ref)
        halo_ref[...] = jnp.zeros_like(halo_ref)

    xn = _rms(h_ref[0], g_ref[...]).astype(BF16)
    zt = lax.dot_general(wt_ref[...], xn, (((1,), (1,)), ((), ())),
                         preferred_element_type=F32)

    f = zt[3 * FOX_WIDTH:3 * FOX_WIDTH + FOX_HEADS] + bf_ref[...]
    ls = jnp.minimum(f, 0.0) - jnp.log1p(jnp.exp(-jnp.abs(f)))
    tri = tri_ref[...]
    c = carry_ref[:, 0:1]
    for part in _split3(ls):
        c = c + jnp.dot(part.astype(BF16), tri, preferred_element_type=F32)
    carry_ref[...] = jnp.broadcast_to(c[:, tm - 1:tm], carry_ref.shape)
    c1, c2, c3 = _split3(c * LOG2E)

    row = lax.broadcasted_iota(jnp.int32, (AUG_ROWS, tm), 0)
    ones_row = jnp.where(
        lax.broadcasted_iota(jnp.int32, (V_ROWS - FOX_HEAD_DIM, ATTN_TILE), 0) == 0, 1.0, 0.0
    ).astype(BF16)
    q_scale = FOX_HEAD_DIM ** -0.5 * LOG2E
    for h in range(FOX_HEADS):
        lo = h * FOX_HEAD_DIM
        c1h, c2h, c3h = c1[h:h + 1], c2[h:h + 1], c3[h:h + 1]
        q = zt[lo:lo + FOX_HEAD_DIM]
        q = q * lax.rsqrt(jnp.mean(q * q, axis=0, keepdims=True) + EPS) * (gq_ref[...] * q_scale)
        aug_q = jnp.where(row == 0, c1h, jnp.where(row == 1, c2h, jnp.where(
            row == 2, c3h, jnp.where(row < 6, 1.0, 0.0))))
        qt_ref[0, h, 0:FOX_HEAD_DIM, :] = q.astype(BF16)
        qt_ref[0, h, FOX_HEAD_DIM:FOX_HEAD_DIM + AUG_ROWS, :] = aug_q.astype(BF16)
        qt_ref[0, h, FOX_HEAD_DIM + AUG_ROWS:, :] = jnp.zeros((QK_PAD, tm), BF16)
        k = zt[FOX_WIDTH + lo:FOX_WIDTH + lo + FOX_HEAD_DIM]
        k = k * lax.rsqrt(jnp.mean(k * k, axis=0, keepdims=True) + EPS) * gk_ref[...]
        aug_k = jnp.where(row < 3, 1.0, jnp.where(row == 3, -c1h, jnp.where(
            row == 4, -c2h, jnp.where(row == 5, -c3h, 0.0))))
        k_full = jnp.concatenate(
            [k, aug_k, jnp.zeros((QK_PAD, tm), F32)], axis=0)
        ka_ref[0, h] = k_full.T.astype(BF16)
        v = zt[2 * FOX_WIDTH + lo:2 * FOX_WIDTH + lo + FOX_HEAD_DIM]
        for jj in range(tm // ATTN_TILE):
            sl = slice(jj * ATTN_TILE, (jj + 1) * ATTN_TILE)
            vt_ref[0, h, jj, 0:FOX_HEAD_DIM, :] = v[:, sl].astype(BF16)
            vt_ref[0, h, jj, FOX_HEAD_DIM:, :] = ones_row

    p = jnp.dot(xn, wp_ref[...], preferred_element_type=F32)
    ext_ref[0:POOL_HALO, :] = halo_ref[...]
    ext_ref[POOL_HALO:POOL_HALO + tm, :] = p
    halo_ref[...] = p[tm - POOL_HALO:tm, :]
    t_pos = si * tm + lax.broadcasted_iota(jnp.int32, (tm, POOL_GROUP_DIM), 0)
    for g, w in enumerate(POOL_WINDOWS):
        lo = g * POOL_GROUP_DIM
        acc = p[:, lo:lo + POOL_GROUP_DIM]
        for j in range(1, w):
            acc = acc + ext_ref[POOL_HALO - j:POOL_HALO - j + tm, lo:lo + POOL_GROUP_DIM]
        cnt = jnp.minimum(t_pos + 1, w).astype(F32)
        mixed = acc / cnt - p[:, lo:lo + POOL_GROUP_DIM]
        y = jnp.dot(mixed.astype(BF16), wpool_ref[g], preferred_element_type=F32)
        pool_ref[0, :, lo:lo + POOL_GROUP_DIM] = (
            y * pscale_ref[:, lo:lo + POOL_GROUP_DIM]).astype(BF16)


def _in_proj(h, g, wt, wp, bf, gq, gk, tri, wpool, pscale):
    B, S, D = h.shape
    tm = IN_TILE
    n_sub = tm // ATTN_TILE
    out_shape = [
        jax.ShapeDtypeStruct((B, FOX_HEADS, QK_AUG, S), BF16),
        jax.ShapeDtypeStruct((B, FOX_HEADS, S, QK_AUG), BF16),
        jax.ShapeDtypeStruct((B, FOX_HEADS, S // ATTN_TILE, V_ROWS, ATTN_TILE), BF16),
        jax.ShapeDtypeStruct((B, S, POOL_WIDTH), BF16),
    ]
    return pl.pallas_call(
        _in_proj_kernel,
        grid=(B, S // tm),
        in_specs=[
            pl.BlockSpec((1, tm, D), lambda b, s: (b, s, 0)),
            _const_spec((1, D)),
            _const_spec(wt.shape),
            _const_spec(wp.shape),
            _const_spec(bf.shape),
            _const_spec(gq.shape),
            _const_spec(gk.shape),
            _const_spec(tri.shape),
            _const_spec(wpool.shape),
            _const_spec(pscale.shape),
        ],
        out_specs=[
            pl.BlockSpec((1, FOX_HEADS, QK_AUG, tm), lambda b, s: (b, 0, 0, s)),
            pl.BlockSpec((1, FOX_HEADS, tm, QK_AUG), lambda b, s: (b, 0, s, 0)),
            pl.BlockSpec((1, FOX_HEADS, n_sub, V_ROWS, ATTN_TILE), lambda b, s: (b, 0, s, 0, 0)),
            pl.BlockSpec((1, tm, POOL_WIDTH), lambda b, s: (b, s, 0)),
        ],
        out_shape=out_shape,
        scratch_shapes=[
            pltpu.VMEM((FOX_HEADS, 128), F32),
            pltpu.VMEM((POOL_HALO, POOL_WIDTH), F32),
            pltpu.VMEM((POOL_HALO + tm, POOL_WIDTH), F32),
        ],
        compiler_params=pltpu.CompilerParams(
            dimension_semantics=("arbitrary", "arbitrary"), vmem_limit_bytes=VMEM_LIMIT),
        name="in_proj",
    )(h, g, wt, wp, bf, gq, gk, tri, wpool, pscale)


def _fox_kernel(qt_ref, ka_ref, vt_ref, o_ref, m_ref, alpha_ref, acc_ref, s_ref):
    n_heads = qt_ref.shape[1]
    t = qt_ref.shape[3]
    i = pl.program_id(1)

    m_ref[...] = jnp.full(m_ref.shape, NEG_BIG, F32)
    acc_ref[...] = jnp.zeros(acc_ref.shape, F32)

    def step(j, n_tiles, masked):
        for h in range(n_heads):
            m_old = m_ref[h]
            m_new = m_old
            for u in range(n_tiles):
                k = ka_ref[0, h, pl.ds(pl.multiple_of((j + u) * t, t), t), :]
                s = jnp.dot(k, qt_ref[0, h], preferred_element_type=F32)
                if masked and u == n_tiles - 1:
                    key = lax.broadcasted_iota(jnp.int32, (t, t), 0)
                    qry = lax.broadcasted_iota(jnp.int32, (t, t), 1)
                    s = jnp.where(key <= qry, s, NEG_BIG)
                s_ref[h, u] = s
                m_new = jnp.maximum(m_new, jnp.max(s, axis=0, keepdims=True))
            alpha_ref[h] = jnp.exp2(m_old - m_new)
            m_ref[h] = m_new
        for h in range(n_heads):
            pv = None
            for u in range(n_tiles):
                p = jnp.exp2(s_ref[h, u] - m_ref[h]).astype(BF16)
                d = jnp.dot(vt_ref[0, h, j + u], p, preferred_element_type=F32)
                pv = d if pv is None else pv + d
            acc_ref[h] = acc_ref[h] * alpha_ref[h] + pv

    def body(jj, carry):
        step(2 * jj, 2, masked=False)
        return carry

    lax.fori_loop(0, i // 2, body, 0)

    @pl.when(i % 2 == 1)
    def _():
        step(i - 1, 2, masked=True)

    @pl.when(i % 2 == 0)
    def _():
        step(i, 1, masked=True)

    for hp in range(n_heads // 2):
        pair = []
        for h in (2 * hp, 2 * hp + 1):
            acc = acc_ref[h]
            pair.append(acc[0:FOX_HEAD_DIM] / acc[FOX_HEAD_DIM:FOX_HEAD_DIM + 1])
        lo = 2 * hp * FOX_HEAD_DIM
        o_ref[0, :, lo:lo + 2 * FOX_HEAD_DIM] = jnp.concatenate(pair, axis=0).T.astype(BF16)


def _fox_attn(qt, ka, vt):
    B, H, _, S = qt.shape
    t = ATTN_TILE
    return pl.pallas_call(
        _fox_kernel,
        grid=(B, S // t),
        in_specs=[
            pl.BlockSpec((1, H, QK_AUG, t), lambda b, i: (b, 0, 0, i)),
            pl.BlockSpec((1, H, S, QK_AUG), lambda b, i: (b, 0, 0, 0), pipeline_mode=pl.Buffered(1)),
            pl.BlockSpec((1, H, S // t, V_ROWS, t), lambda b, i: (b, 0, 0, 0, 0),
                         pipeline_mode=pl.Buffered(1)),
        ],
        out_specs=pl.BlockSpec((1, t, H * FOX_HEAD_DIM), lambda b, i: (b, i, 0)),
        out_shape=jax.ShapeDtypeStruct((B, S, H * FOX_HEAD_DIM), BF16),
        scratch_shapes=[
            pltpu.VMEM((H, 1, t), F32),
            pltpu.VMEM((H, 1, t), F32),
            pltpu.VMEM((H, V_ROWS, t), F32),
            pltpu.VMEM((H, 2, t, t), F32),
        ],
        compiler_params=pltpu.CompilerParams(
            dimension_semantics=("arbitrary", "arbitrary"), vmem_limit_bytes=VMEM_LIMIT),
        name="fox_attn",
    )(qt, ka, vt)


def _post_kernel(h_ref, fox_ref, pool_ref, wout_ref, gmq_ref, wmq_ref, gqm_ref, mk_ref, mv_ref,
                 wmo_ref, gffn_ref, wgu_ref, wd_ref, o_ref, act_ref):
    d_ff = wd_ref.shape[0]
    mix = jnp.concatenate([fox_ref[0], pool_ref[0]], axis=-1)
    h = h_ref[0] + jnp.dot(mix, wout_ref[...], preferred_element_type=F32)

    hn = _rms(h, gmq_ref[...]).astype(BF16)
    mq = jnp.dot(hn, wmq_ref[...], preferred_element_type=F32)
    heads = []
    for hd in range(MEM_HEADS):
        lo = hd * MEM_HEAD_DIM
        qh = _rms(mq[:, lo:lo + MEM_HEAD_DIM], gqm_ref[...] * (MEM_HEAD_DIM ** -0.5 * LOG2E))
        sc = lax.dot_general(qh.astype(BF16), mk_ref[0, :, lo:lo + MEM_HEAD_DIM],
                             (((1,), (1,)), ((), ())), preferred_element_type=F32)
        e = jnp.exp2(sc - jnp.max(sc, axis=-1, keepdims=True))
        pm = (e / jnp.sum(e, axis=-1, keepdims=True)).astype(BF16)
        heads.append(jnp.dot(pm, mv_ref[0, :, lo:lo + MEM_HEAD_DIM],
                             preferred_element_type=F32).astype(BF16))
    mo = jnp.concatenate(heads, axis=-1)
    h = h + jnp.dot(mo, wmo_ref[...], preferred_element_type=F32)

    hn = _rms(h, gffn_ref[...]).astype(BF16)
    for c in range(d_ff // FF_CHUNK):
        lo = c * FF_CHUNK
        gate = jnp.dot(hn, wgu_ref[:, lo:lo + FF_CHUNK], preferred_element_type=F32)
        up = jnp.dot(hn, wgu_ref[:, d_ff + lo:d_ff + lo + FF_CHUNK], preferred_element_type=F32)
        act_ref[:, lo:lo + FF_CHUNK] = (gate * jax.nn.sigmoid(gate) * up).astype(BF16)
    o_ref[0] = h + jnp.dot(act_ref[...], wd_ref[...], preferred_element_type=F32)


def _post(h, fox, pool, wout, gmq, wmq, gqm, mk, mv, wmo, gffn, wgu, wd):
    B, S, D = h.shape
    tm = POST_TILE
    M = mk.shape[1]
    d_ff = wd.shape[0]
    row = lambda b, s: (b, s, 0)
    per_b = lambda b, s: (b, 0, 0)
    return pl.pallas_call(
        _post_kernel,
        grid=(B, S // tm),
        in_specs=[
            pl.BlockSpec((1, tm, D), row),
            pl.BlockSpec((1, tm, FOX_WIDTH), row),
            pl.BlockSpec((1, tm, POOL_WIDTH), row),
            _const_spec(wout.shape),
            _const_spec(gmq.shape),
            _const_spec(wmq.shape),
            _const_spec(gqm.shape),
            pl.BlockSpec((1, M, MEM_WIDTH), per_b),
            pl.BlockSpec((1, M, MEM_WIDTH), per_b),
            _const_spec(wmo.shape),
            _const_spec(gffn.shape),
            _const_spec(wgu.shape),
            _const_spec(wd.shape),
        ],
        out_specs=pl.BlockSpec((1, tm, D), row),
        out_shape=jax.ShapeDtypeStruct((B, S, D), F32),
        scratch_shapes=[pltpu.VMEM((tm, d_ff), BF16)],
        compiler_params=pltpu.CompilerParams(
            dimension_semantics=("arbitrary", "arbitrary"), vmem_limit_bytes=VMEM_LIMIT),
        name="post",
    )(h, fox, pool, wout, gmq, wmq, gqm, mk, mv, wmo, gffn, wgu, wd)


def kernel(x, mem, g_mix, w_in, b_forget, g_q_fox, g_k_fox, w_pool, pool_scale, w_out,
           g_mem_q, g_mem_kv, w_mem_q, w_mem_kv, g_q_mem, g_k_mem, w_mem_out,
           g_ffn, w_gate_up, w_down):
    B, S, D = x.shape
    depth = g_mix.shape[0]
    assert S % IN_TILE == 0 and S % POST_TILE == 0 and IN_TILE % ATTN_TILE == 0
    assert w_down.shape[1] % FF_CHUNK == 0

    idx = jnp.arange(IN_TILE)
    tri = (idx[:, None] <= idx[None, :]).astype(BF16)

    h = x
    for l in range(depth):
        qkv_cols = 3 * FOX_WIDTH
        w_p = w_in[l][:, qkv_cols:qkv_cols + POOL_WIDTH].astype(BF16)
        w_t = jnp.concatenate(
            [w_in[l][:, :qkv_cols], w_in[l][:, qkv_cols + POOL_WIDTH:]], axis=1).T.astype(BF16)

        qt, ka, vt, pool = _in_proj(
            h, g_mix[l][None, :], w_t, w_p, b_forget[l][:, None],
            g_q_fox[l][:, None], g_k_fox[l][:, None], tri,
            w_pool[l].astype(BF16), pool_scale[l][None, :])
        fox = _fox_attn(qt, ka, vt)
        mk, mv = _mem_kv(mem, g_mem_kv[l][None, :], w_mem_kv[l].astype(BF16), g_k_mem[l][None, :])
        h = _post(h, fox, pool, w_out[l].astype(BF16), g_mem_q[l][None, :],
                  w_mem_q[l].astype(BF16), g_q_mem[l][None, :], mk, mv,
                  w_mem_out[l].astype(BF16), g_ffn[l][None, :],
                  w_gate_up[l].astype(BF16), w_down[l].astype(BF16))
    return h
```

```python
import functools
import math

import jax
import jax.numpy as jnp
from jax import lax
from jax.experimental import pallas as pl
from jax.experimental.pallas import tpu as pltpu

F32 = jnp.float32
BF16 = jnp.bfloat16

EPS = 1e-6
FOX_HEADS = 8
FOX_HEAD_DIM = 64
FOX_WIDTH = FOX_HEADS * FOX_HEAD_DIM
POOL_WINDOWS = (2, 4, 8, 16)
POOL_GROUP_DIM = 128
POOL_WIDTH = len(POOL_WINDOWS) * POOL_GROUP_DIM
MEM_HEADS = 4
MEM_HEAD_DIM = 128
MEM_WIDTH = MEM_HEADS * MEM_HEAD_DIM

LOG2E = 1.4426950408889634
NEG_BIG = -1e30

QK_AUG = 128
AUG_ROWS = 16
QK_PAD = QK_AUG - FOX_HEAD_DIM - AUG_ROWS
V_ROWS = 80
POOL_HALO = 16

IN_TILE = 512
ATTN_TILE = 256
POST_TILE = 512
FF_CHUNK = 256
VMEM_LIMIT = 52 * 1024 * 1024


def _rms(x, g):
    return x * lax.rsqrt(jnp.mean(x * x, axis=-1, keepdims=True) + EPS) * g


def _split3(x):
    a = x.astype(BF16).astype(F32)
    r = x - a
    b = r.astype(BF16).astype(F32)
    return a, b, r - b


def _const_spec(shape):
    zeros = (0,) * len(shape)
    return pl.BlockSpec(shape, lambda *_: zeros, pipeline_mode=pl.Buffered(1))


def _mem_kv_kernel(mem_ref, g_ref, w_ref, gk_ref, mk_ref, mv_ref):
    xn = _rms(mem_ref[0], g_ref[...]).astype(BF16)
    kv = jnp.dot(xn, w_ref[...], preferred_element_type=F32)
    for h in range(MEM_HEADS):
        lo = h * MEM_HEAD_DIM
        kh = _rms(kv[:, lo:lo + MEM_HEAD_DIM], gk_ref[...])
        mk_ref[0, :, lo:lo + MEM_HEAD_DIM] = kh.astype(BF16)
    mv_ref[0] = kv[:, MEM_WIDTH:].astype(BF16)


def _mem_kv(mem, g, w, gk):
    B, M, D = mem.shape
    return pl.pallas_call(
        _mem_kv_kernel,
        grid=(B,),
        in_specs=[
            pl.BlockSpec((1, M, D), lambda b: (b, 0, 0)),
            _const_spec((1, D)),
            _const_spec((D, 2 * MEM_WIDTH)),
            _const_spec((1, MEM_HEAD_DIM)),
        ],
        out_specs=[
            pl.BlockSpec((1, M, MEM_WIDTH), lambda b: (b, 0, 0)),
            pl.BlockSpec((1, M, MEM_WIDTH), lambda b: (b, 0, 0)),
        ],
        out_shape=[jax.ShapeDtypeStruct((B, M, MEM_WIDTH), BF16)] * 2,
        compiler_params=pltpu.CompilerParams(dimension_semantics=("arbitrary",)),
        name="mem_kv",
    )(mem, g, w, gk)


def _in_proj_kernel(h_ref, g_ref, wt_ref, wp_ref, bf_ref, gq_ref, gk_ref, tri_ref,
                    wpool_ref, pscale_ref,
                    qt_ref, ka_ref, vt_ref, pool_ref,
                    carry_ref, halo_ref, ext_ref):
    tm = h_ref.shape[1]
    si = pl.program_id(1)

    @pl.when(si == 0)
    def _():
        carry_ref[...] = jnp.zeros_like(carry_ref)
        halo_ref[...] = jnp.zeros_like(halo_ref)

    xn = _rms(h_ref[0], g_ref[...]).astype(BF16)
    zt = lax.dot_general(wt_ref[...], xn, (((1,), (1,)), ((), ())),
                         preferred_element_type=F32)

    f = zt[3 * FOX_WIDTH:3 * FOX_WIDTH + FOX_HEADS] + bf_ref[...]
    ls = jnp.minimum(f, 0.0) - jnp.log1p(jnp.exp(-jnp.abs(f)))
    tri = tri_ref[...]
    c = carry_ref[:, 0:1]
    for part in _split3(ls):
        c = c + jnp.dot(part.astype(BF16), tri, preferred_element_type=F32)
    carry_ref[...] = jnp.broadcast_to(c[:, tm - 1:tm], carry_ref.shape)
    c1, c2, c3 = _split3(c * LOG2E)

    row = lax.broadcasted_iota(jnp.int32, (AUG_ROWS, tm), 0)
    ones_row = jnp.where(
        lax.broadcasted_iota(jnp.int32, (V_ROWS - FOX_HEAD_DIM, ATTN_TILE), 0) == 0, 1.0, 0.0
    ).astype(BF16)
    q_scale = FOX_HEAD_DIM ** -0.5 * LOG2E
    for h in range(FOX_HEADS):
        lo = h * FOX_HEAD_DIM
        c1h, c2h, c3h = c1[h:h + 1], c2[h:h + 1], c3[h:h + 1]
        q = zt[lo:lo + FOX_HEAD_DIM]
        q = q * lax.rsqrt(jnp.mean(q * q, axis=0, keepdims=True) + EPS) * (gq_ref[...] * q_scale)
        aug_q = jnp.where(row == 0, c1h, jnp.where(row == 1, c2h, jnp.where(
            row == 2, c3h, jnp.where(row < 6, 1.0, 0.0))))
        qt_ref[0, h, 0:FOX_HEAD_DIM, :] = q.astype(BF16)
        qt_ref[0, h, FOX_HEAD_DIM:FOX_HEAD_DIM + AUG_ROWS, :] = aug_q.astype(BF16)
        qt_ref[0, h, FOX_HEAD_DIM + AUG_ROWS:, :] = jnp.zeros((QK_PAD, tm), BF16)
        k = zt[FOX_WIDTH + lo:FOX_WIDTH + lo + FOX_HEAD_DIM]
        k = k * lax.rsqrt(jnp.mean(k * k, axis=0, keepdims=True) + EPS) * gk_ref[...]
        aug_k = jnp.where(row < 3, 1.0, jnp.where(row == 3, -c1h, jnp.where(
            row == 4, -c2h, jnp.where(row == 5, -c3h, 0.0))))
        k_full = jnp.concatenate(
            [k, aug_k, jnp.zeros((QK_PAD, tm), F32)], axis=0)
        ka_ref[0, h] = k_full.T.astype(BF16)
        v = zt[2 * FOX_WIDTH + lo:2 * FOX_WIDTH + lo + FOX_HEAD_DIM]
        for jj in range(tm // ATTN_TILE):
            sl = slice(jj * ATTN_TILE, (jj + 1) * ATTN_TILE)
            vt_ref[0, h, jj, 0:FOX_HEAD_DIM, :] = v[:, sl].astype(BF16)
            vt_ref[0, h, jj, FOX_HEAD_DIM:, :] = ones_row

    p = jnp.dot(xn, wp_ref[...], preferred_element_type=F32)
    ext_ref[0:POOL_HALO, :] = halo_ref[...]
    ext_ref[POOL_HALO:POOL_HALO + tm, :] = p
    halo_ref[...] = p[tm - POOL_HALO:tm, :]
    t_pos = si * tm + lax.broadcasted_iota(jnp.int32, (tm, POOL_GROUP_DIM), 0)
    for g, w in enumerate(POOL_WINDOWS):
        lo = g * POOL_GROUP_DIM
        acc = p[:, lo:lo + POOL_GROUP_DIM]
        for j in range(1, w):
            acc = acc + ext_ref[POOL_HALO - j:POOL_HALO - j + tm, lo:lo + POOL_GROUP_DIM]
        cnt = jnp.minimum(t_pos + 1, w).astype(F32)
        mixed = acc / cnt - p[:, lo:lo + POOL_GROUP_DIM]
        y = jnp.dot(mixed.astype(BF16), wpool_ref[g], preferred_element_type=F32)
        pool_ref[0, :, lo:lo + POOL_GROUP_DIM] = (
            y * pscale_ref[:, lo:lo + POOL_GROUP_DIM]).astype(BF16)


def _in_proj(h, g, wt, wp, bf, gq, gk, tri, wpool, pscale):
    B, S, D = h.shape
    tm = IN_TILE
    n_sub = tm // ATTN_TILE
    out_shape = [
        jax.ShapeDtypeStruct((B, FOX_HEADS, QK_AUG, S), BF16),
        jax.ShapeDtypeStruct((B, FOX_HEADS, S, QK_AUG), BF16),
        jax.ShapeDtypeStruct((B, FOX_HEADS, S // ATTN_TILE, V_ROWS, ATTN_TILE), BF16),
        jax.ShapeDtypeStruct((B, S, POOL_WIDTH), BF16),
    ]
    return pl.pallas_call(
        _in_proj_kernel,
        grid=(B, S // tm),
        in_specs=[
            pl.BlockSpec((1, tm, D), lambda b, s: (b, s, 0)),
            _const_spec((1, D)),
            _const_spec(wt.shape),
            _const_spec(wp.shape),
            _const_spec(bf.shape),
            _const_spec(gq.shape),
            _const_spec(gk.shape),
            _const_spec(tri.shape),
            _const_spec(wpool.shape),
            _const_spec(pscale.shape),
        ],
        out_specs=[
            pl.BlockSpec((1, FOX_HEADS, QK_AUG, tm), lambda b, s: (b, 0, 0, s)),
            pl.BlockSpec((1, FOX_HEADS, tm, QK_AUG), lambda b, s: (b, 0, s, 0)),
            pl.BlockSpec((1, FOX_HEADS, n_sub, V_ROWS, ATTN_TILE), lambda b, s: (b, 0, s, 0, 0)),
            pl.BlockSpec((1, tm, POOL_WIDTH), lambda b, s: (b, s, 0)),
        ],
        out_shape=out_shape,
        scratch_shapes=[
            pltpu.VMEM((FOX_HEADS, 128), F32),
            pltpu.VMEM((POOL_HALO, POOL_WIDTH), F32),
            pltpu.VMEM((POOL_HALO + tm, POOL_WIDTH), F32),
        ],
        compiler_params=pltpu.CompilerParams(
            dimension_semantics=("arbitrary", "arbitrary"), vmem_limit_bytes=VMEM_LIMIT),
        name="in_proj",
    )(h, g, wt, wp, bf, gq, gk, tri, wpool, pscale)


def _fox_kernel(qt_ref, ka_ref, vt_ref, o_ref, m_ref, alpha_ref, acc_ref, s_ref):
    n_heads = qt_ref.shape[1]
    t = qt_ref.shape[3]
    i = pl.program_id(1)

    m_ref[...] = jnp.full(m_ref.shape, NEG_BIG, F32)
    acc_ref[...] = jnp.zeros(acc_ref.shape, F32)

    def step(j, n_tiles, masked):
        for h in range(n_heads):
            m_old = m_ref[h]
            m_new = m_old
            for u in range(n_tiles):
                k = ka_ref[0, h, pl.ds(pl.multiple_of((j + u) * t, t), t), :]
                s = jnp.dot(k, qt_ref[0, h], preferred_element_type=F32)
                if masked and u == n_tiles - 1:
                    key = lax.broadcasted_iota(jnp.int32, (t, t), 0)
                    qry = lax.broadcasted_iota(jnp.int32, (t, t), 1)
                    s = jnp.where(key <= qry, s, NEG_BIG)
                s_ref[h, u] = s
                m_new = jnp.maximum(m_new, jnp.max(s, axis=0, keepdims=True))
            alpha_ref[h] = jnp.exp2(m_old - m_new)
            m_ref[h] = m_new
        for h in range(n_heads):
            pv = None
            for u in range(n_tiles):
                p = jnp.exp2(s_ref[h, u] - m_ref[h]).astype(BF16)
                d = jnp.dot(vt_ref[0, h, j + u], p, preferred_element_type=F32)
                pv = d if pv is None else pv + d
            acc_ref[h] = acc_ref[h] * alpha_ref[h] + pv

    def body(jj, carry):
        step(2 * jj, 2, masked=False)
        return carry

    lax.fori_loop(0, i // 2, body, 0)

    @pl.when(i % 2 == 1)
    def _():
        step(i - 1, 2, masked=True)

    @pl.when(i % 2 == 0)
    def _():
        step(i, 1, masked=True)

    for hp in range(n_heads // 2):
        pair = []
        for h in (2 * hp, 2 * hp + 1):
            acc = acc_ref[h]
            pair.append(acc[0:FOX_HEAD_DIM] / acc[FOX_HEAD_DIM:FOX_HEAD_DIM + 1])
        lo = 2 * hp * FOX_HEAD_DIM
        o_ref[0, :, lo:lo + 2 * FOX_HEAD_DIM] = jnp.concatenate(pair, axis=0).T.astype(BF16)


def _fox_attn(qt, ka, vt):
    B, H, _, S = qt.shape
    t = ATTN_TILE
    return pl.pallas_call(
        _fox_kernel,
        grid=(B, S // t),
        in_specs=[
            pl.BlockSpec((1, H, QK_AUG, t), lambda b, i: (b, 0, 0, i)),
            pl.BlockSpec((1, H, S, QK_AUG), lambda b, i: (b, 0, 0, 0), pipeline_mode=pl.Buffered(1)),
            pl.BlockSpec((1, H, S // t, V_ROWS, t), lambda b, i: (b, 0, 0, 0, 0),
                         pipeline_mode=pl.Buffered(1)),
        ],
        out_specs=pl.BlockSpec((1, t, H * FOX_HEAD_DIM), lambda b, i: (b, i, 0)),
        out_shape=jax.ShapeDtypeStruct((B, S, H * FOX_HEAD_DIM), BF16),
        scratch_shapes=[
            pltpu.VMEM((H, 1, t), F32),
            pltpu.VMEM((H, 1, t), F32),
            pltpu.VMEM((H, V_ROWS, t), F32),
            pltpu.VMEM((H, 2, t, t), F32),
        ],
        compiler_params=pltpu.CompilerParams(
            dimension_semantics=("arbitrary", "arbitrary"), vmem_limit_bytes=VMEM_LIMIT),
        name="fox_attn",
    )(qt, ka, vt)


def _post_kernel(h_ref, fox_ref, pool_ref, wout_ref, gmq_ref, wmq_ref, gqm_ref, mk_ref, mv_ref,
                 wmo_ref, gffn_ref, wgu_ref, wd_ref, o_ref, act_ref):
    d_ff = wd_ref.shape[0]
    mix = jnp.concatenate([fox_ref[0], pool_ref[0]], axis=-1)
    h = h_ref[0] + jnp.dot(mix, wout_ref[...], preferred_element_type=F32)

    hn = _rms(h, gmq_ref[...]).astype(BF16)
    mq = jnp.dot(hn, wmq_ref[...], preferred_element_type=F32)
    heads = []
    for hd in range(MEM_HEADS):
        lo = hd * MEM_HEAD_DIM
        qh = _rms(mq[:, lo:lo + MEM_HEAD_DIM], gqm_ref[...] * (MEM_HEAD_DIM ** -0.5 * LOG2E))
        sc = lax.dot_general(qh.astype(BF16), mk_ref[0, :, lo:lo + MEM_HEAD_DIM],
                             (((1,), (1,)), ((), ())), preferred_element_type=F32)
        e = jnp.exp2(sc - jnp.max(sc, axis=-1, keepdims=True))
        pm = (e / jnp.sum(e, axis=-1, keepdims=True)).astype(BF16)
        heads.append(jnp.dot(pm, mv_ref[0, :, lo:lo + MEM_HEAD_DIM],
                             preferred_element_type=F32).astype(BF16))
    mo = jnp.concatenate(heads, axis=-1)
    h = h + jnp.dot(mo, wmo_ref[...], preferred_element_type=F32)

    hn = _rms(h, gffn_ref[...]).astype(BF16)
    for c in range(d_ff // FF_CHUNK):
        lo = c * FF_CHUNK
        gate = jnp.dot(hn, wgu_ref[:, lo:lo + FF_CHUNK], preferred_element_type=F32)
        up = jnp.dot(hn, wgu_ref[:, d_ff + lo:d_ff + lo + FF_CHUNK], preferred_element_type=F32)
        act_ref[:, lo:lo + FF_CHUNK] = (gate * jax.nn.sigmoid(gate) * up).astype(BF16)
    o_ref[0] = h + jnp.dot(act_ref[...], wd_ref[...], preferred_element_type=F32)


def _post(h, fox, pool, wout, gmq, wmq, gqm, mk, mv, wmo, gffn, wgu, wd):
    B, S, D = h.shape
    tm = POST_TILE
    M = mk.shape[1]
    d_ff = wd.shape[0]
    row = lambda b, s: (b, s, 0)
    per_b = lambda b, s: (b, 0, 0)
    return pl.pallas_call(
        _post_kernel,
        grid=(B, S // tm),
        in_specs=[
            pl.BlockSpec((1, tm, D), row),
            pl.BlockSpec((1, tm, FOX_WIDTH), row),
            pl.BlockSpec((1, tm, POOL_WIDTH), row),
            _const_spec(wout.shape),
            _const_spec(gmq.shape),
            _const_spec(wmq.shape),
            _const_spec(gqm.shape),
            pl.BlockSpec((1, M, MEM_WIDTH), per_b),
            pl.BlockSpec((1, M, MEM_WIDTH), per_b),
            _const_spec(wmo.shape),
            _const_spec(gffn.shape),
            _const_spec(wgu.shape),
            _const_spec(wd.shape),
        ],
        out_specs=pl.BlockSpec((1, tm, D), row),
        out_shape=jax.ShapeDtypeStruct((B, S, D), F32),
        scratch_shapes=[pltpu.VMEM((tm, d_ff), BF16)],
        compiler_params=pltpu.CompilerParams(
            dimension_semantics=("arbitrary", "arbitrary"), vmem_limit_bytes=VMEM_LIMIT),
        name="post",
    )(h, fox, pool, wout, gmq, wmq, gqm, mk, mv, wmo, gffn, wgu, wd)


def kernel(x, mem, g_mix, w_in, b_forget, g_q_fox, g_k_fox, w_pool, pool_scale, w_out,
           g_mem_q, g_mem_kv, w_mem_q, w_mem_kv, g_q_mem, g_k_mem, w_mem_out,
           g_ffn, w_gate_up, w_down):
    B, S, D = x.shape
    depth = g_mix.shape[0]
    assert S % IN_TILE == 0 and S % POST_TILE == 0 and IN_TILE % ATTN_TILE == 0
    assert w_down.shape[1] % FF_CHUNK == 0

    idx = jnp.arange(IN_TILE)
    tri = (idx[:, None] <= idx[None, :]).astype(BF16)

    h = x
    for l in range(depth):
        qkv_cols = 3 * FOX_WIDTH
        w_p = w_in[l][:, qkv_cols:qkv_cols + POOL_WIDTH].astype(BF16)
        w_t = jnp.concatenate(
            [w_in[l][:, :qkv_cols], w_in[l][:, qkv_cols + POOL_WIDTH:]], axis=1).T.astype(BF16)

        qt, ka, vt, pool = _in_proj(
            h, g_mix[l][None, :], w_t, w_p, b_forget[l][:, None],
            g_q_fox[l][:, None], g_k_fox[l][:, None], tri,
            w_pool[l].astype(BF16), pool_scale[l][None, :])
        fox = _fox_attn(qt, ka, vt)
        mk, mv = _mem_kv(mem, g_mem_kv[l][None, :], w_mem_kv[l].astype(BF16), g_k_mem[l][None, :])
        h = _post(h, fox, pool, w_out[l].astype(BF16), g_mem_q[l][None, :],
                  w_mem_q[l].astype(BF16), g_q_mem[l][None, :], mk, mv,
                  w_mem_out[l].astype(BF16), g_ffn[l][None, :],
                  w_gate_up[l].astype(BF16), w_down[l].astype(BF16))
    return h
```

```python
import functools
import math

import jax
import jax.numpy as jnp
from jax import lax
from jax.experimental import pallas as pl
from jax.experimental.pallas import tpu as pltpu

F32 = jnp.float32
BF16 = jnp.bfloat16

EPS = 1e-6
FOX_HEADS = 8
FOX_HEAD_DIM = 64
FOX_WIDTH = FOX_HEADS * FOX_HEAD_DIM
POOL_WINDOWS = (2, 4, 8, 16)
POOL_GROUP_DIM = 128
POOL_WIDTH = len(POOL_WINDOWS) * POOL_GROUP_DIM
MEM_HEADS = 4
MEM_HEAD_DIM = 128
MEM_WIDTH = MEM_HEADS * MEM_HEAD_DIM

LOG2E = 1.4426950408889634
NEG_BIG = -1e30

QK_AUG = 128
AUG_ROWS = 16
QK_PAD = QK_AUG - FOX_HEAD_DIM - AUG_ROWS
V_ROWS = 80
POOL_HALO = 16

IN_TILE = 512
ATTN_TILE = 256
Q_HALVES = 2
POST_TILE = 512
FF_CHUNK = 256
VMEM_LIMIT = 52 * 1024 * 1024


def _rms(x, g):
    return x * lax.rsqrt(jnp.mean(x * x, axis=-1, keepdims=True) + EPS) * g


def _split3(x):
    a = x.astype(BF16).astype(F32)
    r = x - a
    b = r.astype(BF16).astype(F32)
    return a, b, r - b


def _const_spec(shape):
    zeros = (0,) * len(shape)
    return pl.BlockSpec(shape, lambda *_: zeros, pipeline_mode=pl.Buffered(1))


def _mem_kv_kernel(mem_ref, g_ref, w_ref, gk_ref, mk_ref, mv_ref):
    xn = _rms(mem_ref[0], g_ref[...]).astype(BF16)
    kv = jnp.dot(xn, w_ref[...], preferred_element_type=F32)
    for h in range(MEM_HEADS):
        lo = h * MEM_HEAD_DIM
        kh = _rms(kv[:, lo:lo + MEM_HEAD_DIM], gk_ref[...])
        mk_ref[0, :, lo:lo + MEM_HEAD_DIM] = kh.astype(BF16)
    mv_ref[0] = kv[:, MEM_WIDTH:].astype(BF16)


def _mem_kv(mem, g, w, gk):
    B, M, D = mem.shape
    return pl.pallas_call(
        _mem_kv_kernel,
        grid=(B,),
        in_specs=[
            pl.BlockSpec((1, M, D), lambda b: (b, 0, 0)),
            _const_spec((1, D)),
            _const_spec((D, 2 * MEM_WIDTH)),
            _const_spec((1, MEM_HEAD_DIM)),
        ],
        out_specs=[
            pl.BlockSpec((1, M, MEM_WIDTH), lambda b: (b, 0, 0)),
            pl.BlockSpec((1, M, MEM_WIDTH), lambda b: (b, 0, 0)),
        ],
        out_shape=[jax.ShapeDtypeStruct((B, M, MEM_WIDTH), BF16)] * 2,
        compiler_params=pltpu.CompilerParams(dimension_semantics=("arbitrary",)),
        name="mem_kv",
    )(mem, g, w, gk)


def _in_proj_kernel(h_ref, g_ref, wt_ref, wp_ref, bf_ref, gq_ref, gk_ref, tri_ref,
                    wpool_ref, pscale_ref,
                    qt_ref, ka_ref, vt_ref, pool_ref,
                    carry_ref, halo_ref, ext_ref):
    tm = h_ref.shape[1]
    si = pl.program_id(1)

    @pl.when(si == 0)
    def _():
        carry_ref[...] = jnp.zeros_like(carry_ref)
        halo_ref[...] = jnp.zeros_like(halo_ref)

    xn = _rms(h_ref[0], g_ref[...]).astype(BF16)
    zt = lax.dot_general(wt_ref[...], xn, (((1,), (1,)), ((), ())),
                         preferred_element_type=F32)

    f = zt[3 * FOX_WIDTH:3 * FOX_WIDTH + FOX_HEADS] + bf_ref[...]
    ls = jnp.minimum(f, 0.0) - jnp.log1p(jnp.exp(-jnp.abs(f)))
    tri = tri_ref[...]
    c = carry_ref[:, 0:1]
    for part in _split3(ls):
        c = c + jnp.dot(part.astype(BF16), tri, preferred_element_type=F32)
    carry_ref[...] = jnp.broadcast_to(c[:, tm - 1:tm], carry_ref.shape)
    c1, c2, c3 = _split3(c * LOG2E)

    row = lax.broadcasted_iota(jnp.int32, (AUG_ROWS, tm), 0)
    ones_row = jnp.where(
        lax.broadcasted_iota(jnp.int32, (V_ROWS - FOX_HEAD_DIM, ATTN_TILE), 0) == 0, 1.0, 0.0
    ).astype(BF16)
    q_scale = FOX_HEAD_DIM ** -0.5 * LOG2E
    for h in range(FOX_HEADS):
        lo = h * FOX_HEAD_DIM
        c1h, c2h, c3h = c1[h:h + 1], c2[h:h + 1], c3[h:h + 1]
        q = zt[lo:lo + FOX_HEAD_DIM]
        q = q * lax.rsqrt(jnp.mean(q * q, axis=0, keepdims=True) + EPS) * (gq_ref[...] * q_scale)
        aug_q = jnp.where(row == 0, c1h, jnp.where(row == 1, c2h, jnp.where(
            row == 2, c3h, jnp.where(row < 6, 1.0, 0.0))))
        qt_ref[0, h, 0:FOX_HEAD_DIM, :] = q.astype(BF16)
        qt_ref[0, h, FOX_HEAD_DIM:FOX_HEAD_DIM + AUG_ROWS, :] = aug_q.astype(BF16)
        qt_ref[0, h, FOX_HEAD_DIM + AUG_ROWS:, :] = jnp.zeros((QK_PAD, tm), BF16)
        k = zt[FOX_WIDTH + lo:FOX_WIDTH + lo + FOX_HEAD_DIM]
        k = k * lax.rsqrt(jnp.mean(k * k, axis=0, keepdims=True) + EPS) * gk_ref[...]
        aug_k = jnp.where(row < 3, 1.0, jnp.where(row == 3, -c1h, jnp.where(
            row == 4, -c2h, jnp.where(row == 5, -c3h, 0.0))))
        k_full = jnp.concatenate(
            [k, aug_k, jnp.zeros((QK_PAD, tm), F32)], axis=0)
        ka_ref[0, h] = k_full.T.astype(BF16)
        v = zt[2 * FOX_WIDTH + lo:2 * FOX_WIDTH + lo + FOX_HEAD_DIM]
        for jj in range(tm // ATTN_TILE):
            sl = slice(jj * ATTN_TILE, (jj + 1) * ATTN_TILE)
            vt_ref[0, h, jj, 0:FOX_HEAD_DIM, :] = v[:, sl].astype(BF16)
            vt_ref[0, h, jj, FOX_HEAD_DIM:, :] = ones_row

    p = jnp.dot(xn, wp_ref[...], preferred_element_type=F32)
    ext_ref[0:POOL_HALO, :] = halo_ref[...]
    ext_ref[POOL_HALO:POOL_HALO + tm, :] = p
    halo_ref[...] = p[tm - POOL_HALO:tm, :]
    t_pos = si * tm + lax.broadcasted_iota(jnp.int32, (tm, POOL_GROUP_DIM), 0)
    for g, w in enumerate(POOL_WINDOWS):
        lo = g * POOL_GROUP_DIM
        acc = p[:, lo:lo + POOL_GROUP_DIM]
        for j in range(1, w):
            acc = acc + ext_ref[POOL_HALO - j:POOL_HALO - j + tm, lo:lo + POOL_GROUP_DIM]
        cnt = jnp.minimum(t_pos + 1, w).astype(F32)
        mixed = acc / cnt - p[:, lo:lo + POOL_GROUP_DIM]
        y = jnp.dot(mixed.astype(BF16), wpool_ref[g], preferred_element_type=F32)
        pool_ref[0, :, lo:lo + POOL_GROUP_DIM] = (
            y * pscale_ref[:, lo:lo + POOL_GROUP_DIM]).astype(BF16)


def _in_proj(h, g, wt, wp, bf, gq, gk, tri, wpool, pscale):
    B, S, D = h.shape
    tm = IN_TILE
    n_sub = tm // ATTN_TILE
    out_shape = [
        jax.ShapeDtypeStruct((B, FOX_HEADS, QK_AUG, S), BF16),
        jax.ShapeDtypeStruct((B, FOX_HEADS, S, QK_AUG), BF16),
        jax.ShapeDtypeStruct((B, FOX_HEADS, S // ATTN_TILE, V_ROWS, ATTN_TILE), BF16),
        jax.ShapeDtypeStruct((B, S, POOL_WIDTH), BF16),
    ]
    return pl.pallas_call(
        _in_proj_kernel,
        grid=(B, S // tm),
        in_specs=[
            pl.BlockSpec((1, tm, D), lambda b, s: (b, s, 0)),
            _const_spec((1, D)),
            _const_spec(wt.shape),
            _const_spec(wp.shape),
            _const_spec(bf.shape),
            _const_spec(gq.shape),
            _const_spec(gk.shape),
            _const_spec(tri.shape),
            _const_spec(wpool.shape),
            _const_spec(pscale.shape),
        ],
        out_specs=[
            pl.BlockSpec((1, FOX_HEADS, QK_AUG, tm), lambda b, s: (b, 0, 0, s)),
            pl.BlockSpec((1, FOX_HEADS, tm, QK_AUG), lambda b, s: (b, 0, s, 0)),
            pl.BlockSpec((1, FOX_HEADS, n_sub, V_ROWS, ATTN_TILE), lambda b, s: (b, 0, s, 0, 0)),
            pl.BlockSpec((1, tm, POOL_WIDTH), lambda b, s: (b, s, 0)),
        ],
        out_shape=out_shape,
        scratch_shapes=[
            pltpu.VMEM((FOX_HEADS, 128), F32),
            pltpu.VMEM((POOL_HALO, POOL_WIDTH), F32),
            pltpu.VMEM((POOL_HALO + tm, POOL_WIDTH), F32),
        ],
        compiler_params=pltpu.CompilerParams(
            dimension_semantics=("arbitrary", "arbitrary"), vmem_limit_bytes=VMEM_LIMIT),
        name="in_proj",
    )(h, g, wt, wp, bf, gq, gk, tri, wpool, pscale)


def _fox_kernel(qt_ref, ka_ref, vt_ref, o_ref, m_ref, alpha_ref, acc_ref, s_ref):
    n_heads = qt_ref.shape[1]
    t = ATTN_TILE
    i = pl.program_id(1)

    m_ref[...] = jnp.full(m_ref.shape, NEG_BIG, F32)
    acc_ref[...] = jnp.zeros(acc_ref.shape, F32)

    def step(j, chains):
        for h in range(n_heads):
            for c, n_tiles, masked in chains:
                cols = slice(c * t, (c + 1) * t)
                m_old = m_ref[h, :, cols]
                m_new = m_old
                for u in range(n_tiles):
                    k = ka_ref[0, h, pl.ds(pl.multiple_of((j + u) * t, t), t), :]
                    s = jnp.dot(k, qt_ref[0, h, :, cols], preferred_element_type=F32)
                    if masked and u == n_tiles - 1:
                        key = lax.broadcasted_iota(jnp.int32, (t, t), 0)
                        qry = lax.broadcasted_iota(jnp.int32, (t, t), 1)
                        s = jnp.where(key <= qry, s, NEG_BIG)
                    s_ref[h, c, u] = s
                    m_new = jnp.maximum(m_new, jnp.max(s, axis=0, keepdims=True))
                alpha_ref[h, :, cols] = jnp.exp2(m_old - m_new)
                m_ref[h, :, cols] = m_new
        for h in range(n_heads):
            for c, n_tiles, masked in chains:
                cols = slice(c * t, (c + 1) * t)
                pv = None
                for u in range(n_tiles):
                    p = jnp.exp2(s_ref[h, c, u] - m_ref[h, :, cols]).astype(BF16)
                    d = jnp.dot(vt_ref[0, h, j + u], p, preferred_element_type=F32)
                    pv = d if pv is None else pv + d
                acc_ref[h, :, cols] = acc_ref[h, :, cols] * alpha_ref[h, :, cols] + pv

    def body(jj, carry):
        step(2 * jj, ((0, 2, False), (1, 2, False)))
        return carry

    lax.fori_loop(0, i, body, 0)
    step(2 * i, ((0, 1, True), (1, 2, True)))

    for hp in range(n_heads // 2):
        pair = []
        for h in (2 * hp, 2 * hp + 1):
            acc = acc_ref[h]
            pair.append(acc[0:FOX_HEAD_DIM] / acc[FOX_HEAD_DIM:FOX_HEAD_DIM + 1])
        lo = 2 * hp * FOX_HEAD_DIM
        o_ref[0, :, lo:lo + 2 * FOX_HEAD_DIM] = jnp.concatenate(pair, axis=0).T.astype(BF16)


def _fox_attn(qt, ka, vt):
    B, H, _, S = qt.shape
    t = ATTN_TILE
    tq = Q_HALVES * t
    return pl.pallas_call(
        _fox_kernel,
        grid=(B, S // tq),
        in_specs=[
            pl.BlockSpec((1, H, QK_AUG, tq), lambda b, i: (b, 0, 0, i)),
            pl.BlockSpec((1, H, S, QK_AUG), lambda b, i: (b, 0, 0, 0), pipeline_mode=pl.Buffered(1)),
            pl.BlockSpec((1, H, S // t, V_ROWS, t), lambda b, i: (b, 0, 0, 0, 0),
                         pipeline_mode=pl.Buffered(1)),
        ],
        out_specs=pl.BlockSpec((1, tq, H * FOX_HEAD_DIM), lambda b, i: (b, i, 0)),
        out_shape=jax.ShapeDtypeStruct((B, S, H * FOX_HEAD_DIM), BF16),
        scratch_shapes=[
            pltpu.VMEM((H, 1, tq), F32),
            pltpu.VMEM((H, 1, tq), F32),
            pltpu.VMEM((H, V_ROWS, tq), F32),
            pltpu.VMEM((H, Q_HALVES, 2, t, t), F32),
        ],
        compiler_params=pltpu.CompilerParams(
            dimension_semantics=("arbitrary", "arbitrary"), vmem_limit_bytes=VMEM_LIMIT),
        name="fox_attn",
    )(qt, ka, vt)


def _post_kernel(h_ref, fox_ref, pool_ref, wout_ref, gmq_ref, wmq_ref, gqm_ref, mk_ref, mv_ref,
                 wmo_ref, gffn_ref, wgu_ref, wd_ref, o_ref, act_ref):
    d_ff = wd_ref.shape[0]
    mix = jnp.concatenate([fox_ref[0], pool_ref[0]], axis=-1)
    h = h_ref[0] + jnp.dot(mix, wout_ref[...], preferred_element_type=F32)

    hn = _rms(h, gmq_ref[...]).astype(BF16)
    mq = jnp.dot(hn, wmq_ref[...], preferred_element_type=F32)
    heads = []
    for hd in range(MEM_HEADS):
        lo = hd * MEM_HEAD_DIM
        qh = _rms(mq[:, lo:lo + MEM_HEAD_DIM], gqm_ref[...] * (MEM_HEAD_DIM ** -0.5 * LOG2E))
        sc = lax.dot_general(qh.astype(BF16), mk_ref[0, :, lo:lo + MEM_HEAD_DIM],
                             (((1,), (1,)), ((), ())), preferred_element_type=F32)
        e = jnp.exp2(sc - jnp.max(sc, axis=-1, keepdims=True))
        pm = (e / jnp.sum(e, axis=-1, keepdims=True)).astype(BF16)
        heads.append(jnp.dot(pm, mv_ref[0, :, lo:lo + MEM_HEAD_DIM],
                             preferred_element_type=F32).astype(BF16))
    mo = jnp.concatenate(heads, axis=-1)
    h = h + jnp.dot(mo, wmo_ref[...], preferred_element_type=F32)

    hn = _rms(h, gffn_ref[...]).astype(BF16)
    for c in range(d_ff // FF_CHUNK):
        lo = c * FF_CHUNK
        gate = jnp.dot(hn, wgu_ref[:, lo:lo + FF_CHUNK], preferred_element_type=F32)
        up = jnp.dot(hn, wgu_ref[:, d_ff + lo:d_ff + lo + FF_CHUNK], preferred_element_type=F32)
        act_ref[:, lo:lo + FF_CHUNK] = (gate * jax.nn.sigmoid(gate) * up).astype(BF16)
    o_ref[0] = h + jnp.dot(act_ref[...], wd_ref[...], preferred_element_type=F32)


def _post(h, fox, pool, wout, gmq, wmq, gqm, mk, mv, wmo, gffn, wgu, wd):
    B, S, D = h.shape
    tm = POST_TILE
    M = mk.shape[1]
    d_ff = wd.shape[0]
    row = lambda b, s: (b, s, 0)
    per_b = lambda b, s: (b, 0, 0)
    return pl.pallas_call(
        _post_kernel,
        grid=(B, S // tm),
        in_specs=[
            pl.BlockSpec((1, tm, D), row),
            pl.BlockSpec((1, tm, FOX_WIDTH), row),
            pl.BlockSpec((1, tm, POOL_WIDTH), row),
            _const_spec(wout.shape),
            _const_spec(gmq.shape),
            _const_spec(wmq.shape),
            _const_spec(gqm.shape),
            pl.BlockSpec((1, M, MEM_WIDTH), per_b),
            pl.BlockSpec((1, M, MEM_WIDTH), per_b),
            _const_spec(wmo.shape),
            _const_spec(gffn.shape),
            _const_spec(wgu.shape),
            _const_spec(wd.shape),
        ],
        out_specs=pl.BlockSpec((1, tm, D), row),
        out_shape=jax.ShapeDtypeStruct((B, S, D), F32),
        scratch_shapes=[pltpu.VMEM((tm, d_ff), BF16)],
        compiler_params=pltpu.CompilerParams(
            dimension_semantics=("arbitrary", "arbitrary"), vmem_limit_bytes=VMEM_LIMIT),
        name="post",
    )(h, fox, pool, wout, gmq, wmq, gqm, mk, mv, wmo, gffn, wgu, wd)


def kernel(x, mem, g_mix, w_in, b_forget, g_q_fox, g_k_fox, w_pool, pool_scale, w_out,
           g_mem_q, g_mem_kv, w_mem_q, w_mem_kv, g_q_mem, g_k_mem, w_mem_out,
           g_ffn, w_gate_up, w_down):
    B, S, D = x.shape
    depth = g_mix.shape[0]
    assert S % IN_TILE == 0 and S % POST_TILE == 0 and IN_TILE % ATTN_TILE == 0
    assert S % (Q_HALVES * ATTN_TILE) == 0
    assert w_down.shape[1] % FF_CHUNK == 0

    idx = jnp.arange(IN_TILE)
    tri = (idx[:, None] <= idx[None, :]).astype(BF16)

    h = x
    for l in range(depth):
        qkv_cols = 3 * FOX_WIDTH
        w_p = w_in[l][:, qkv_cols:qkv_cols + POOL_WIDTH].astype(BF16)
        w_t = jnp.concatenate(
            [w_in[l][:, :qkv_cols], w_in[l][:, qkv_cols + POOL_WIDTH:]], axis=1).T.astype(BF16)

        qt, ka, vt, pool = _in_proj(
            h, g_mix[l][None, :], w_t, w_p, b_forget[l][:, None],
            g_q_fox[l][:, None], g_k_fox[l][:, None], tri,
            w_pool[l].astype(BF16), pool_scale[l][None, :])
        fox = _fox_attn(qt, ka, vt)
        mk, mv = _mem_kv(mem, g_mem_kv[l][None, :], w_mem_kv[l].astype(BF16), g_k_mem[l][None, :])
        h = _post(h, fox, pool, w_out[l].astype(BF16), g_mem_q[l][None, :],
                  w_mem_q[l].astype(BF16), g_q_mem[l][None, :], mk, mv,
                  w_mem_out[l].astype(BF16), g_ffn[l][None, :],
                  w_gate_up[l].astype(BF16), w_down[l].astype(BF16))
    return h
```
